```python
import math
import jax
import jax.numpy as jnp
from jax import lax
import numpy as np

D_MODEL = 2048
BATCH = 2
SEQ = 4096
DEPTH = 2
DEC_BATCH = 128
DEC_SEQ = 8
PAST_LEN = 16384
PAGE_SIZE = 128

MLA_HEADS = D_MODEL // 256
MLA_NOPE = 128
MLA_ROPE = 32
MLA_QK = MLA_NOPE + MLA_ROPE
MLA_V = 128
KV_RANK = 128
FOX_HEADS = D_MODEL // 512
FOX_DIM = 128
FORGET_BIAS = 4.0
DIFF_HEADS = D_MODEL // 512
DIFF_QK = 64
DIFF_V = 128
FFN_HIDDEN = -(-8 * D_MODEL // (3 * 256)) * 256
PLE_DIM = 256
ROPE_THETA = 10000.0
Q_BLOCK = 128
NORM_EPS = 1e-6
IN_SIZES = (MLA_HEADS * MLA_QK, KV_RANK, MLA_ROPE,
            FOX_HEADS * FOX_DIM, FOX_DIM, FOX_DIM, FOX_HEADS,
            DIFF_HEADS * 2 * DIFF_QK, 2 * DIFF_QK, DIFF_V)
N_IN = sum(IN_SIZES)
MIX_WIDTH = MLA_HEADS * MLA_V + FOX_HEADS * FOX_DIM + DIFF_HEADS * DIFF_V

kernel_name = 'hybrid_mla_fox_diff_decode_step'


def rmsnorm(x, g):
    xf = x.astype(jnp.float32)
    y = xf * lax.rsqrt(jnp.mean(xf * xf, axis=-1, keepdims=True) + NORM_EPS)
    return (y * g.astype(jnp.float32)).astype(x.dtype)


def rope(x, pos):
    half = x.shape[-1] // 2
    inv_freq = jnp.exp(-math.log(ROPE_THETA) * jnp.arange(half, dtype=jnp.float32) / half)
    ang = pos.astype(jnp.float32)[:, None] * inv_freq[None, :]
    cos = jnp.cos(ang)[:, None, :]
    sin = jnp.sin(ang)[:, None, :]
    xf = x.astype(jnp.float32)
    x1, x2 = xf[..., :half], xf[..., half:]
    return jnp.concatenate([x1 * cos - x2 * sin, x2 * cos + x1 * sin], axis=-1).astype(x.dtype)


def sweep_queries(fn, q_args, qpos):
    S = qpos.shape[0]
    blk = Q_BLOCK if S % Q_BLOCK == 0 else S
    nb = S // blk

    def split(a):
        return jnp.moveaxis(a.reshape((a.shape[0], nb, blk) + a.shape[2:]), 1, 0)

    blocks = tuple(split(a) for a in q_args) + (qpos.reshape(nb, blk),)
    out = lax.map(lambda t: fn(t[:-1], t[-1]), blocks)
    out = jnp.moveaxis(out, 0, 1)
    return out.reshape((out.shape[0], S) + out.shape[3:])


def mla_attend(q_lat, q_pe, c, kpe, qpos, kpos):
    scale = MLA_QK ** -0.5

    def blockfn(qs, qp):
        ql, qr = qs
        s = (jnp.einsum('bqhr,bkr->bhqk', ql, c) + jnp.einsum('bqhp,bkp->bhqk', qr, kpe)).astype(jnp.float32) * scale
        s = jnp.where(kpos[None, :] <= qp[:, None], s, -jnp.inf)
        w = jax.nn.softmax(s, axis=-1).astype(c.dtype)
        return jnp.einsum('bhqk,bkr->bqhr', w, c)

    return sweep_queries(blockfn, (q_lat, q_pe), qpos)


def fox_attend(q, k, v, f_q, f_k, qpos, kpos):
    scale = FOX_DIM ** -0.5
    f_kt = jnp.transpose(f_k, (0, 2, 1))[:, :, None, :]

    def blockfn(qs, qp):
        qb, fq = qs
        s = jnp.einsum('bqhd,bkd->bhqk', qb, k).astype(jnp.float32) * scale
        s = s + jnp.transpose(fq, (0, 2, 1))[..., None] - f_kt
        s = jnp.where(kpos[None, :] <= qp[:, None], s, -jnp.inf)
        w = jax.nn.softmax(s, axis=-1).astype(v.dtype)
        return jnp.einsum('bhqk,bkd->bqhd', w, v)

    return sweep_queries(blockfn, (q, f_q), qpos)


def diff_attend(q, k, v, lam, qpos, kpos):
    scale = DIFF_QK ** -0.5

    def blockfn(qs, qp):
        (qb,) = qs
        s = jnp.einsum('bqhcd,bkcd->bchqk', qb, k).astype(jnp.float32) * scale
        s = jnp.where(kpos[None, :] <= qp[:, None], s, -jnp.inf)
        w = jax.nn.softmax(s, axis=-1)
        w = (w[:, 0] - lam * w[:, 1]).astype(v.dtype)
        return jnp.einsum('bhqk,bkd->bqhd', w, v)

    return sweep_queries(blockfn, (q,), qpos)


def gather_pages(cache, i, page_table):
    g = cache[i, page_table]
    return g.reshape((g.shape[0], g.shape[1] * g.shape[2]) + g.shape[3:])


def decoder_layer(i, h, p, past, w):
    (g_attn, w_in, g_mla_q, g_ckv, g_kpe, w_uk, w_uv, g_fox_q, g_fox_k, b_fox,
     g_diff_q, g_diff_k, lam_q1, lam_k1, lam_q2, lam_k2, g_diff_sub, w_out,
     g_ffn, w_gate, w_up, w_down, g_ple, w_ple_gate, w_ple_proj) = w
    B, S, _ = h.shape
    past_len = 0 if past is None else past[0].shape[1]
    qpos = past_len + jnp.arange(S, dtype=jnp.int32)
    kpos = jnp.arange(past_len + S, dtype=jnp.int32)

    def with_past(j, new):
        return new if past is None else jnp.concatenate([past[j], new], axis=1)

    u = rmsnorm(h, g_attn)
    proj = u @ w_in
    (q_mla, ckv, kpe_raw, q_fox, k_fox, v_fox, f_fox,
     q_dif, k_dif, v_dif) = jnp.split(proj, list(np.cumsum(IN_SIZES)[:-1]), axis=-1)

    q = rmsnorm(q_mla.reshape(B, S, MLA_HEADS, MLA_QK), g_mla_q)
    q_nope, q_pe = q[..., :MLA_NOPE], rope(q[..., MLA_NOPE:], qpos)
    c_new = rmsnorm(ckv, g_ckv)
    kpe_new = rope(rmsnorm(kpe_raw, g_kpe)[:, :, None, :], qpos)[:, :, 0]
    q_lat = jnp.einsum('bshn,rhn->bshr', q_nope, w_uk)
    o_lat = mla_attend(q_lat, q_pe, with_past(0, c_new), with_past(1, kpe_new), qpos, kpos)
    o_mla = jnp.einsum('bshr,rhv->bshv', o_lat, w_uv).reshape(B, S, MLA_HEADS * MLA_V)

    qf = rmsnorm(q_fox.reshape(B, S, FOX_HEADS, FOX_DIM), g_fox_q)
    kf_new = rmsnorm(k_fox, g_fox_k)
    vf_new = v_fox
    logf_new = jax.nn.log_sigmoid((f_fox + b_fox).astype(jnp.float32))
    f_cum = lax.cumsum(with_past(4, logf_new).astype(jnp.float32), axis=1)
    o_fox = fox_attend(qf, with_past(2, kf_new), with_past(3, vf_new),
                       f_cum[:, -S:], f_cum, qpos, kpos).reshape(B, S, FOX_HEADS * FOX_DIM)

    lam_init = 0.8 - 0.6 * math.exp(-0.3 * i)
    qd = rmsnorm(q_dif.reshape(B, S, DIFF_HEADS * 2, DIFF_QK), g_diff_q)
    qd = rope(qd, qpos).reshape(B, S, DIFF_HEADS, 2, DIFF_QK)
    kd_new = rope(rmsnorm(k_dif.reshape(B, S, 2, DIFF_QK), g_diff_k), qpos)
    vd_new = v_dif
    lam = (jnp.exp(jnp.sum(lam_q1.astype(jnp.float32) * lam_k1.astype(jnp.float32)))
           - jnp.exp(jnp.sum(lam_q2.astype(jnp.float32) * lam_k2.astype(jnp.float32))) + lam_init)
    o_d = diff_attend(qd, with_past(5, kd_new), with_past(6, vd_new), lam, qpos, kpos)
    o_dif = (rmsnorm(o_d, g_diff_sub) * (1.0 - lam_init)).reshape(B, S, DIFF_HEADS * DIFF_V)

    h = h + jnp.concatenate([o_mla, o_fox, o_dif], axis=-1) @ w_out

    u = rmsnorm(h, g_ffn)
    h = h + (jax.nn.silu(u @ w_gate) * (u @ w_up)) @ w_down

    h = h + jax.nn.sigmoid(rmsnorm(h, g_ple) @ w_ple_gate) * (p @ w_ple_proj)
    return h, (c_new, kpe_new, kf_new, vf_new, logf_new, kd_new, vd_new)


def setup_inputs(seed: int = 0) -> dict:
    key = jax.random.key(seed)
    ks = list(jax.random.split(key, 48))
    f32 = jnp.float32

    def normal(shape, scale=1.0):
        return jax.random.normal(ks.pop(), shape, f32) * scale

    def gain(shape):
        return 1.0 + 0.05 * jax.random.normal(ks.pop(), shape, f32)

    n_pages = PAST_LEN // PAGE_SIZE
    n_used = DEC_BATCH * n_pages
    n_pool = n_used + (n_used + 3) // 4
    pool = (DEPTH, n_pool, PAGE_SIZE)
    perm = jax.random.permutation(ks.pop(), n_pool)
    page_table = perm[:n_used].reshape(DEC_BATCH, n_pages).astype(jnp.int32)
    D = D_MODEL
    return {
        'x_prompt': normal((BATCH, SEQ, D)),
        'x_sample': normal((DEC_BATCH, DEC_SEQ, D)),
        'cache_mla_ckv': normal(pool + (KV_RANK,)),
        'cache_mla_kpe': normal(pool + (MLA_ROPE,)),
        'cache_fox_k': normal(pool + (FOX_DIM,)),
        'cache_fox_v': normal(pool + (FOX_DIM,)),
        'cache_fox_logf': jax.nn.log_sigmoid(FORGET_BIAS + normal(pool + (FOX_HEADS,), 0.5)),
        'cache_diff_k': normal(pool + (2, DIFF_QK)),
        'cache_diff_v': normal(pool + (DIFF_V,)),
        'page_table': page_table,
        'p_prompt': normal((DEPTH, BATCH, SEQ, PLE_DIM)),
        'p_sample': normal((DEPTH, DEC_BATCH, DEC_SEQ, PLE_DIM)),
        'g_attn': gain((DEPTH, D)),
        'w_in': normal((DEPTH, D, N_IN), D ** -0.5),
        'g_mla_q': gain((DEPTH, MLA_QK)),
        'g_ckv': gain((DEPTH, KV_RANK)),
        'g_kpe': gain((DEPTH, MLA_ROPE)),
        'w_uk': normal((DEPTH, KV_RANK, MLA_HEADS, MLA_NOPE), KV_RANK ** -0.5),
        'w_uv': normal((DEPTH, KV_RANK, MLA_HEADS, MLA_V), KV_RANK ** -0.5),
        'g_fox_q': gain((DEPTH, FOX_DIM)),
        'g_fox_k': gain((DEPTH, FOX_DIM)),
        'b_fox': FORGET_BIAS + normal((DEPTH, FOX_HEADS), 0.5),
        'g_diff_q': gain((DEPTH, DIFF_QK)),
        'g_diff_k': gain((DEPTH, DIFF_QK)),
        'lam_q1': normal((DEPTH, DIFF_QK), 0.1),
        'lam_k1': normal((DEPTH, DIFF_QK), 0.1),
        'lam_q2': normal((DEPTH, DIFF_QK), 0.1),
        'lam_k2': normal((DEPTH, DIFF_QK), 0.1),
        'g_diff_sub': gain((DEPTH, DIFF_V)),
        'w_out': normal((DEPTH, MIX_WIDTH, D), MIX_WIDTH ** -0.5),
        'g_ffn': gain((DEPTH, D)),
        'w_gate': normal((DEPTH, D, FFN_HIDDEN), D ** -0.5),
        'w_up': normal((DEPTH, D, FFN_HIDDEN), D ** -0.5),
        'w_down': normal((DEPTH, FFN_HIDDEN, D), FFN_HIDDEN ** -0.5),
        'g_ple': gain((DEPTH, D)),
        'w_ple_gate': normal((DEPTH, D, D), D ** -0.5),
        'w_ple_proj': normal((DEPTH, PLE_DIM, D), PLE_DIM ** -0.5),
    }


def reference(x_prompt, x_sample, cache_mla_ckv, cache_mla_kpe, cache_fox_k, cache_fox_v,
              cache_fox_logf, cache_diff_k, cache_diff_v, page_table, p_prompt, p_sample,
              g_attn, w_in, g_mla_q, g_ckv, g_kpe, w_uk, w_uv, g_fox_q, g_fox_k, b_fox,
              g_diff_q, g_diff_k, lam_q1, lam_k1, lam_q2, lam_k2, g_diff_sub, w_out,
              g_ffn, w_gate, w_up, w_down, g_ple, w_ple_gate, w_ple_proj):
    hp, hs = x_prompt, x_sample
    rows_p, rows_s = [], []
    caches = (cache_mla_ckv, cache_mla_kpe, cache_fox_k, cache_fox_v,
              cache_fox_logf, cache_diff_k, cache_diff_v)
    for i in range(DEPTH):
        w = (g_attn[i], w_in[i], g_mla_q[i], g_ckv[i], g_kpe[i], w_uk[i], w_uv[i],
             g_fox_q[i], g_fox_k[i], b_fox[i], g_diff_q[i], g_diff_k[i],
             lam_q1[i], lam_k1[i], lam_q2[i], lam_k2[i], g_diff_sub[i], w_out[i],
             g_ffn[i], w_gate[i], w_up[i], w_down[i], g_ple[i], w_ple_gate[i], w_ple_proj[i])
        hp, rp = decoder_layer(i, hp, p_prompt[i], None, w)
        past = tuple(gather_pages(c, i, page_table) for c in caches)
        hs, rs = decoder_layer(i, hs, p_sample[i], past, w)
        rows_p.append(rp)
        rows_s.append(rs)
    new_p = [jnp.stack([r[j] for r in rows_p]) for j in range(7)]
    new_s = [jnp.stack([r[j] for r in rows_s]) for j in range(7)]
    return (hp, hs, new_p[0], new_s[0], new_p[1], new_s[1], new_p[2], new_s[2],
            new_p[3], new_s[3], new_p[4], new_s[4], new_p[5], new_s[5], new_p[6], new_s[6])
```

```python
import functools
import math

import jax
import jax.numpy as jnp
from jax import lax
from jax.experimental import pallas as pl
from jax.experimental.pallas import tpu as pltpu

_F32 = jnp.float32
_BF16 = jnp.bfloat16
_LANE = 128
_VMEM_LIMIT = 52 * 1024 * 1024
_NORM_EPS = 1e-6
_ROPE_THETA = 10000.0
_NEG_INF = float("-inf")


def _cparams(n_grid):
    return pltpu.CompilerParams(dimension_semantics=("arbitrary",) * n_grid,
                                vmem_limit_bytes=_VMEM_LIMIT)


def _rms(x, n):
    ss = jnp.sum(x * x, axis=-1, keepdims=True)
    return x * lax.rsqrt(ss * (1.0 / n) + _NORM_EPS)


def _rope(x, cos, sin_a, sin_b, half):
    lanes = x.shape[-1]
    return x * cos + pltpu.roll(x, half, 1) * sin_a + pltpu.roll(x, lanes - half, 1) * sin_b


def _split3(x):
    hi = x.astype(_BF16)
    r1 = x - hi.astype(_F32)
    mid = r1.astype(_BF16)
    lo = (r1 - mid.astype(_F32)).astype(_BF16)
    return hi, mid, lo


def _tri(n, kind):
    r = lax.broadcasted_iota(jnp.int32, (n, n), 0)
    c = lax.broadcasted_iota(jnp.int32, (n, n), 1)
    if kind == "lower_incl":
        m = c <= r
    else:
        m = r > c
    return jnp.where(m, 1.0, 0.0).astype(_BF16)


def _online_update(s, v_bf, m_sc, l_sc, acc_sc):
    m_prev = m_sc[...]
    m_new = jnp.maximum(m_prev, jnp.max(s, axis=-1, keepdims=True))
    alpha = jnp.exp(m_prev - m_new)
    p = jnp.exp(s - m_new)
    l_sc[...] = alpha * l_sc[...] + jnp.sum(p, axis=-1, keepdims=True)
    acc_sc[...] = alpha * acc_sc[...] + jnp.dot(p.astype(_BF16), v_bf, preferred_element_type=_F32)
    m_sc[...] = m_new


def _nt_dot(a, b):
    return lax.dot_general(a, b, (((1,), (1,)), ((), ())), preferred_element_type=_F32)


def _inproj_kernel(x_ref, gattn_ref, wm_ref, wf_ref, wd_ref, wuk_ref, gv_ref, tab_ref,
                   qm_ref, qf_ref, qd_ref, c_ref, kpe_ref, kf_ref, vf_ref, logf_ref, kd_ref, vd_ref,
                   *, hm, hf, hd, mla_qk, mla_rope, dqk):
    x = x_ref[...]
    u = (_rms(x, x.shape[-1]) * gattn_ref[...]).astype(_BF16)
    g_nope, g_rope, g_ckv, g_kpe = gv_ref[0:1], gv_ref[1:2], gv_ref[2:3], gv_ref[3:4]
    b_fox, g_fq, g_fk, g_dq, g_dk = gv_ref[4:5], gv_ref[5:6], gv_ref[6:7], gv_ref[7:8], gv_ref[8:9]
    cos_m, sin_ma, sin_mb = tab_ref[0], tab_ref[1], tab_ref[2]
    cos_d, sin_da, sin_db = tab_ref[3], tab_ref[4], tab_ref[5]
    half_m, half_d = mla_rope // 2, dqk // 2

    pm = jnp.dot(u, wm_ref[...], preferred_element_type=_F32)
    for h in range(hm):
        nope = pm[:, 2 * _LANE * h:2 * _LANE * h + _LANE]
        rp = pm[:, 2 * _LANE * h + _LANE:2 * _LANE * (h + 1)]
        ss = jnp.sum(nope * nope, axis=-1, keepdims=True) + jnp.sum(rp * rp, axis=-1, keepdims=True)
        r = lax.rsqrt(ss * (1.0 / mla_qk) + _NORM_EPS)
        nn = nope * r * g_nope
        rr = _rope(rp * r * g_rope, cos_m, sin_ma, sin_mb, half_m)
        qm_ref[h, :, 0:_LANE] = jnp.dot(nn.astype(_BF16), wuk_ref[h], preferred_element_type=_F32)
        qm_ref[h, :, _LANE:2 * _LANE] = rr
    base = 2 * _LANE * hm
    c_ref[...] = _rms(pm[:, base:base + _LANE], _LANE) * g_ckv
    kr = _rms(pm[:, base + _LANE:base + 2 * _LANE], mla_rope) * g_kpe
    kpe_ref[...] = _rope(kr, cos_m, sin_ma, sin_mb, half_m)

    pf = jnp.dot(u, wf_ref[...], preferred_element_type=_F32)
    z = pf[:, 0:_LANE] + b_fox
    logf_ref[...] = jnp.minimum(z, 0.0) - jnp.log1p(jnp.exp(-jnp.abs(z)))
    for h in range(hf):
        qf_ref[h] = _rms(pf[:, _LANE * (1 + h):_LANE * (2 + h)], _LANE) * g_fq
    base = _LANE * (1 + hf)
    kf_ref[...] = _rms(pf[:, base:base + _LANE], _LANE) * g_fk
    vf_ref[...] = pf[:, base + _LANE:base + 2 * _LANE]

    pd = jnp.dot(u, wd_ref[...], preferred_element_type=_F32)
    for j in range(2 * hd):
        qn = _rms(pd[:, _LANE * j:_LANE * (j + 1)], dqk) * g_dq
        qd_ref[j] = _rope(qn, cos_d, sin_da, sin_db, half_d)
    base = _LANE * 2 * hd
    k = pd[:, base:base + _LANE]
    first = lax.broadcasted_iota(jnp.int32, k.shape, 1) < dqk
    k2 = k * k
    ss0 = jnp.sum(jnp.where(first, k2, 0.0), axis=-1, keepdims=True)
    ss1 = jnp.sum(jnp.where(first, 0.0, k2), axis=-1, keepdims=True)
    r = lax.rsqrt(jnp.where(first, ss0, ss1) * (1.0 / dqk) + _NORM_EPS)
    kd_ref[...] = _rope(k * r * g_dk, cos_d, sin_da, sin_db, half_d)
    vd_ref[...] = pd[:, base + _LANE:base + 2 * _LANE]


def _inproj(x, gattn, wm, wf, wd, wuk, gv, tabs, *, tm, hm, hf, hd, mla_qk, mla_rope, dqk):
    t, d = x.shape
    const2 = lambda i: (0, 0)
    const3 = lambda i: (0, 0, 0)
    row = lambda i: (i, 0)
    hrow = lambda i: (0, i, 0)
    w_spec = lambda w: pl.BlockSpec(w.shape, const2 if w.ndim == 2 else const3,
                                    pipeline_mode=pl.Buffered(1))
    tok = pl.BlockSpec((tm, _LANE), row)
    tok_shape = jax.ShapeDtypeStruct((t, _LANE), _F32)
    kern = functools.partial(_inproj_kernel, hm=hm, hf=hf, hd=hd, mla_qk=mla_qk,
                             mla_rope=mla_rope, dqk=dqk)
    return pl.pallas_call(
        kern,
        grid=(t // tm,),
        in_specs=[pl.BlockSpec((tm, d), row), w_spec(gattn), w_spec(wm), w_spec(wf), w_spec(wd),
                  w_spec(wuk), w_spec(gv), pl.BlockSpec((tabs.shape[0], tm, _LANE), hrow)],
        out_specs=[pl.BlockSpec((hm, tm, 2 * _LANE), hrow), pl.BlockSpec((hf, tm, _LANE), hrow),
                   pl.BlockSpec((2 * hd, tm, _LANE), hrow)] + [tok] * 7,
        out_shape=[jax.ShapeDtypeStruct((hm, t, 2 * _LANE), _F32),
                   jax.ShapeDtypeStruct((hf, t, _LANE), _F32),
                   jax.ShapeDtypeStruct((2 * hd, t, _LANE), _F32)] + [tok_shape] * 7,
        compiler_params=_cparams(1),
        name="inproj",
    )(x, gattn, wm, wf, wd, wuk, gv, tabs)


def _cumsum_kernel(x_ref, col_ref, row_ref, carry_sc):
    @pl.when(pl.program_id(1) == 0)
    def _():
        carry_sc[...] = jnp.zeros_like(carry_sc)

    n = x_ref.shape[0]
    tri = _tri(n, "lower_incl")
    cum = carry_sc[...] + sum(jnp.dot(tri, p, preferred_element_type=_F32) for p in _split3(x_ref[...]))
    col_ref[...] = cum
    carry_sc[...] = cum[n - 1:n, :]
    row_ref[0] = cum.T[0:row_ref.shape[1], :]


def _prompt_cumsum(logf, *, batch, seq, blk):
    nb = seq // blk
    return pl.pallas_call(
        _cumsum_kernel,
        grid=(batch, nb),
        in_specs=[pl.BlockSpec((blk, _LANE), lambda b, i: (b * nb + i, 0))],
        out_specs=[pl.BlockSpec((blk, _LANE), lambda b, i: (b * nb + i, 0)),
                   pl.BlockSpec((1, 8, blk), lambda b, i: (b, 0, i))],
        out_shape=[jax.ShapeDtypeStruct((batch * seq, _LANE), _F32),
                   jax.ShapeDtypeStruct((batch, 8, seq), _F32)],
        scratch_shapes=[pltpu.VMEM((1, _LANE), _F32)],
        compiler_params=_cparams(2),
        name="prompt_cumsum",
    )(logf)


def _causal_mask(s, g, tq, tk):
    s3 = s.reshape(g, tq, tk)
    row = lax.broadcasted_iota(jnp.int32, (1, tq, tk), 1)
    col = lax.broadcasted_iota(jnp.int32, (1, tq, tk), 2)
    return jnp.where(col <= row, s3, _NEG_INF).reshape(g * tq, tk)


def _flash_init(m_sc, l_sc, acc_sc):
    m_sc[...] = jnp.full_like(m_sc, _NEG_INF)
    l_sc[...] = jnp.zeros_like(l_sc)
    acc_sc[...] = jnp.zeros_like(acc_sc)


def _pa_mla_kernel(q_ref, c_ref, kpe_ref, o_ref, m_sc, l_sc, acc_sc, *, scale):
    qi, ki = pl.program_id(1), pl.program_id(2)
    g, tq, dq = q_ref.shape
    tk = c_ref.shape[0]

    @pl.when(ki == 0)
    def _():
        _flash_init(m_sc, l_sc, acc_sc)

    def step(diag):
        q = q_ref[...].reshape(g * tq, dq).astype(_BF16)
        c = c_ref[...]
        k = jnp.concatenate([c, kpe_ref[...]], axis=-1).astype(_BF16)
        s = _nt_dot(q, k) * scale
        if diag:
            s = _causal_mask(s, g, tq, tk)
        _online_update(s, c.astype(_BF16), m_sc, l_sc, acc_sc)

    @pl.when(ki < qi)
    def _():
        step(False)

    @pl.when(ki == qi)
    def _():
        step(True)
        o_ref[...] = (acc_sc[...] / l_sc[...]).reshape(o_ref.shape)


def _pa_fox_kernel(q_ref, k_ref, v_ref, fq_ref, fk_ref, o_ref, m_sc, l_sc, acc_sc, *, scale):
    qi, ki = pl.program_id(1), pl.program_id(2)
    g, tq, dq = q_ref.shape
    tk = k_ref.shape[0]

    @pl.when(ki == 0)
    def _():
        _flash_init(m_sc, l_sc, acc_sc)

    def step(diag):
        q = q_ref[...].reshape(g * tq, dq).astype(_BF16)
        s = _nt_dot(q, k_ref[...].astype(_BF16)) * scale
        fq = fq_ref[...]
        fk = fk_ref[0]
        s = jnp.concatenate(
            [s[h * tq:(h + 1) * tq] + (fq[:, h:h + 1] - fk[h:h + 1, :]) for h in range(g)], axis=0)
        if diag:
            s = _causal_mask(s, g, tq, tk)
        _online_update(s, v_ref[...].astype(_BF16), m_sc, l_sc, acc_sc)

    @pl.when(ki < qi)
    def _():
        step(False)

    @pl.when(ki == qi)
    def _():
        step(True)
        o_ref[...] = (acc_sc[...] / l_sc[...]).reshape(o_ref.shape)


def _diff_lambda(lam_ref, lam_init):
    a = jnp.sum(lam_ref[0:1] * lam_ref[1:2], axis=-1, keepdims=True)
    b = jnp.sum(lam_ref[2:3] * lam_ref[3:4], axis=-1, keepdims=True)
    return jnp.exp(a) - jnp.exp(b) + lam_init


def _diff_combine(acc, l, lam, half):
    w = acc / l
    return w[0:half] - lam * w[half:2 * half]


def _pa_diff_kernel(q_ref, k_ref, v_ref, lam_ref, o_ref, m_sc, l_sc, acc_sc, *, scale, lam_init):
    qi, ki = pl.program_id(1), pl.program_id(2)
    g, tq, dq = q_ref.shape
    tk = k_ref.shape[0]

    @pl.when(ki == 0)
    def _():
        _flash_init(m_sc, l_sc, acc_sc)

    def step(diag):
        q = q_ref[...].reshape(g * tq, dq).astype(_BF16)
        s = _nt_dot(q, k_ref[...].astype(_BF16)) * scale
        if diag:
            s = _causal_mask(s, g, tq, tk)
        _online_update(s, v_ref[...].astype(_BF16), m_sc, l_sc, acc_sc)

    @pl.when(ki < qi)
    def _():
        step(False)

    @pl.when(ki == qi)
    def _():
        step(True)
        lam = _diff_lambda(lam_ref, lam_init)
        o_ref[...] = _diff_combine(acc_sc[...], l_sc[...], lam, (g // 2) * tq).reshape(o_ref.shape)


def _prompt_attn(kern, q, kv_list, extra, *, batch, seq, tb, g_out):
    g, _, dq = q.shape
    nb = seq // tb
    kv_spec = pl.BlockSpec((tb, _LANE), lambda b, qi, ki: (b * nb + jnp.minimum(ki, qi), 0))
    return pl.pallas_call(
        kern,
        grid=(batch, nb, nb),
        in_specs=[pl.BlockSpec((g, tb, dq), lambda b, qi, ki: (0, b * nb + qi, 0))]
                 + [kv_spec] * len(kv_list) + [s for _, s in extra],
        out_specs=pl.BlockSpec((g_out, tb, _LANE), lambda b, qi, ki: (0, b * nb + qi, 0)),
        out_shape=jax.ShapeDtypeStruct((g_out, batch * seq, _LANE), _F32),
        scratch_shapes=[pltpu.VMEM((g * tb, 1), _F32), pltpu.VMEM((g * tb, 1), _F32),
                        pltpu.VMEM((g * tb, _LANE), _F32)],
        compiler_params=_cparams(3),
        name=kern.func.__name__.strip("_"),
    )(q, *kv_list, *[a for a, _ in extra])


def _page_copies(pt_ref, chunk, slot, layer, cp, reverse, n_pages, hbm_refs, bufs, sems):
    copies = []
    chunks_per_seq = n_pages // cp
    for i in range(cp):
        if reverse:
            b = chunk // chunks_per_seq
            c = chunk % chunks_per_seq
            idx = b * n_pages + (n_pages - 1 - (c * cp + i))
        else:
            idx = chunk * cp + i
        page = pt_ref[idx]
        for a, (hbm, buf) in enumerate(zip(hbm_refs, bufs)):
            copies.append(pltpu.make_async_copy(hbm.at[layer, page], buf.at[slot, i], sems.at[slot, a]))
    return copies


def _decode_pipeline(pt_ref, layer, cp, reverse, n_pages, hbm_refs, bufs, sems):
    nc = pl.num_programs(1)
    g = pl.program_id(0) * nc + pl.program_id(1)
    total = pl.num_programs(0) * nc
    slot = g % 2
    args = (layer, cp, reverse, n_pages, hbm_refs, bufs, sems)

    @pl.when(g == 0)
    def _():
        for cpy in _page_copies(pt_ref, g, slot, *args):
            cpy.start()

    @pl.when(g + 1 < total)
    def _():
        for cpy in _page_copies(pt_ref, g + 1, 1 - slot, *args):
            cpy.start()

    for cpy in _page_copies(pt_ref, g, slot, *args):
        cpy.wait()
    return slot


def _new_token_mask(s, n_new):
    t = lax.broadcasted_iota(jnp.int32, s.shape, 0) % n_new
    col = lax.broadcasted_iota(jnp.int32, s.shape, 1)
    return jnp.where(col <= t, s, _NEG_INF)


def _pad_rows(x, rows):
    return jnp.concatenate([x, jnp.zeros((rows - x.shape[0], x.shape[1]), x.dtype)], axis=0)


def _dec_mla_kernel(pt_ref, q_ref, cn_ref, kn_ref, ckv_hbm, kpe_hbm, o_ref,
                    ckv_buf, kpe_buf, sems, m_sc, l_sc, acc_sc, *, layer, cp, n_pages, scale, rope):
    c_idx = pl.program_id(1)
    g, n_new, dq = q_ref.shape
    slot = _decode_pipeline(pt_ref, layer, cp, False, n_pages, (ckv_hbm, kpe_hbm),
                            (ckv_buf, kpe_buf), sems)
    q = q_ref[...].reshape(g * n_new, dq)
    q_lat = q[:, 0:_LANE].astype(_BF16)
    q_pe = q[:, _LANE:_LANE + rope].astype(_BF16)

    @pl.when(c_idx == 0)
    def _():
        _flash_init(m_sc, l_sc, acc_sc)
        cn = _pad_rows(cn_ref[...], _LANE)
        kn = jnp.concatenate([cn, _pad_rows(kn_ref[...], _LANE)], axis=-1).astype(_BF16)
        s = _new_token_mask(_nt_dot(q.astype(_BF16), kn) * scale, n_new)
        _online_update(s, cn.astype(_BF16), m_sc, l_sc, acc_sc)

    c = ckv_buf[slot].reshape(cp * _LANE, _LANE).astype(_BF16)
    s_pe = jnp.concatenate(
        [jnp.dot(q_pe, kpe_buf[slot, i].astype(_BF16), preferred_element_type=_F32) for i in range(cp)],
        axis=-1)
    s = (_nt_dot(q_lat, c) + s_pe) * scale
    _online_update(s, c, m_sc, l_sc, acc_sc)

    @pl.when(c_idx == pl.num_programs(1) - 1)
    def _():
        o_ref[...] = (acc_sc[...] / l_sc[...]).reshape(o_ref.shape)


def _dec_fox_kernel(pt_ref, q_ref, kn_ref, vn_ref, fn_ref, k_hbm, v_hbm, f_hbm, o_ref,
                    k_buf, v_buf, f_buf, sems, m_sc, l_sc, acc_sc, carry_sc, ncum_sc,
                    *, layer, cp, n_pages, scale):
    c_idx = pl.program_id(1)
    g, n_new, dq = q_ref.shape
    slot = _decode_pipeline(pt_ref, layer, cp, True, n_pages, (k_hbm, v_hbm, f_hbm),
                            (k_buf, v_buf, f_buf), sems)
    q = q_ref[...].reshape(g * n_new, dq).astype(_BF16)

    def head_rows(per_head):
        return jnp.concatenate([jnp.broadcast_to(per_head(h), (n_new, per_head(h).shape[-1]))
                                for h in range(g)], axis=0)

    @pl.when(c_idx == 0)
    def _():
        _flash_init(m_sc, l_sc, acc_sc)
        carry_sc[...] = jnp.zeros_like(carry_sc)
        fn = _pad_rows(fn_ref[...], _LANE)
        ncol = sum(jnp.dot(_tri(_LANE, "lower_incl"), p, preferred_element_type=_F32)
                   for p in _split3(fn))
        ncum_sc[...] = ncol[0:n_new, :]
        nrow = ncol.T
        kn = _pad_rows(kn_ref[...], _LANE).astype(_BF16)
        s = _nt_dot(q, kn) * scale
        s = s + head_rows(lambda h: ncol[0:n_new, h:h + 1] - nrow[h:h + 1, :])
        s = _new_token_mask(s, n_new)
        _online_update(s, _pad_rows(vn_ref[...], _LANE).astype(_BF16), m_sc, l_sc, acc_sc)

    k = k_buf[slot].reshape(cp * _LANE, _LANE).astype(_BF16)
    s = _nt_dot(q, k) * scale
    sgt = _tri(_LANE, "row_gt_col")
    run = carry_sc[...]
    biases = []
    for i in range(cp):
        f = f_buf[slot, i]
        within = sum(jnp.dot(p, sgt, preferred_element_type=_F32) for p in _split3(f))
        biases.append(run + within)
        run = run + jnp.sum(f, axis=-1, keepdims=True)
    carry_sc[...] = run
    bias = jnp.concatenate(biases, axis=-1)
    ncum = ncum_sc[...]
    s = s + head_rows(lambda h: bias[h:h + 1, :]) + jnp.concatenate(
        [ncum[:, h:h + 1] for h in range(g)], axis=0)
    _online_update(s, v_buf[slot].reshape(cp * _LANE, _LANE).astype(_BF16), m_sc, l_sc, acc_sc)

    @pl.when(c_idx == pl.num_programs(1) - 1)
    def _():
        o_ref[...] = (acc_sc[...] / l_sc[...]).reshape(o_ref.shape)


def _dec_diff_kernel(pt_ref, q_ref, kn_ref, vn_ref, lam_ref, k_hbm, v_hbm, o_ref,
                     k_buf, v_buf, sems, m_sc, l_sc, acc_sc,
                     *, layer, cp, n_pages, scale, lam_init):
    c_idx = pl.program_id(1)
    g, n_new, dq = q_ref.shape
    slot = _decode_pipeline(pt_ref, layer, cp, False, n_pages, (k_hbm, v_hbm), (k_buf, v_buf), sems)
    q = q_ref[...].reshape(g * n_new, dq).astype(_BF16)

    @pl.when(c_idx == 0)
    def _():
        _flash_init(m_sc, l_sc, acc_sc)
        kn = _pad_rows(kn_ref[...], _LANE).astype(_BF16)
        s = _new_token_mask(_nt_dot(q, kn) * scale, n_new)
        _online_update(s, _pad_rows(vn_ref[...], _LANE).astype(_BF16), m_sc, l_sc, acc_sc)

    s = jnp.concatenate(
        [jnp.dot(q, k_buf[slot, i].astype(_BF16), preferred_element_type=_F32) for i in range(cp)],
        axis=-1) * scale
    _online_update(s, v_buf[slot].reshape(cp * _LANE, _LANE).astype(_BF16), m_sc, l_sc, acc_sc)

    @pl.when(c_idx == pl.num_programs(1) - 1)
    def _():
        lam = _diff_lambda(lam_ref, lam_init)
        o_ref[...] = _diff_combine(acc_sc[...], l_sc[...], lam, (g // 2) * n_new).reshape(o_ref.shape)


def _decode_attn(kern, page_table, q, new_rows, extra, caches, extra_scratch, *,
                 tok0, n_new, cp, g_out):
    g, _, dq = q.shape
    dec_batch, n_pages = page_table.shape
    blk0 = tok0 // n_new
    new_spec = pl.BlockSpec((n_new, _LANE), lambda b, c, pt: (blk0 + b, 0))
    grid_spec = pltpu.PrefetchScalarGridSpec(
        num_scalar_prefetch=1,
        grid=(dec_batch, n_pages // cp),
        in_specs=[pl.BlockSpec((g, n_new, dq), lambda b, c, pt: (0, blk0 + b, 0))]
                 + [new_spec] * len(new_rows) + [s for _, s in extra]
                 + [pl.BlockSpec(memory_space=pl.ANY)] * len(caches),
        out_specs=pl.BlockSpec((g_out, n_new, _LANE), lambda b, c, pt: (0, b, 0)),
        scratch_shapes=[pltpu.VMEM((2, cp) + c.shape[2:], _F32) for c in caches]
                       + [pltpu.SemaphoreType.DMA((2, len(caches))),
                          pltpu.VMEM((g * n_new, 1), _F32), pltpu.VMEM((g * n_new, 1), _F32),
                          pltpu.VMEM((g * n_new, _LANE), _F32)] + list(extra_scratch),
    )
    return pl.pallas_call(
        kern,
        grid_spec=grid_spec,
        out_shape=jax.ShapeDtypeStruct((g_out, dec_batch * n_new, _LANE), _F32),
        compiler_params=_cparams(2),
        name=kern.func.__name__.strip("_"),
    )(page_table.reshape(-1), q, *new_rows, *[a for a, _ in extra], *caches)


def _merge_kernel(h_ref, om_ref, of_ref, od_ref, wuv_ref, gsub_ref, wout_ref, o_ref, *, sub_scale):
    parts = []
    for h in range(om_ref.shape[0]):
        parts.append(jnp.dot(om_ref[h].astype(_BF16), wuv_ref[h],
                             preferred_element_type=_F32).astype(_BF16))
    for h in range(of_ref.shape[0]):
        parts.append(of_ref[h].astype(_BF16))
    for h in range(od_ref.shape[0]):
        parts.append((_rms(od_ref[h], _LANE) * gsub_ref[...] * sub_scale).astype(_BF16))
    mix = jnp.concatenate(parts, axis=-1)
    o_ref[...] = h_ref[...] + jnp.dot(mix, wout_ref[...], preferred_element_type=_F32)


def _merge(h, om, of, od, wuv, gsub, wout, *, tm, sub_scale):
    t, d = h.shape
    row = lambda i: (i, 0)
    hrow = lambda i: (0, i, 0)
    const = lambda a: pl.BlockSpec(a.shape, (lambda i: (0,) * a.ndim), pipeline_mode=pl.Buffered(1))
    return pl.pallas_call(
        functools.partial(_merge_kernel, sub_scale=sub_scale),
        grid=(t // tm,),
        in_specs=[pl.BlockSpec((tm, d), row)]
                 + [pl.BlockSpec((a.shape[0], tm, _LANE), hrow) for a in (om, of, od)]
                 + [const(wuv), const(gsub), const(wout)],
        out_specs=pl.BlockSpec((tm, d), row),
        out_shape=jax.ShapeDtypeStruct((t, d), _F32),
        compiler_params=_cparams(1),
        name="merge",
    )(h, om, of, od, wuv, gsub, wout)


def _ffn_kernel(x_ref, g_ref, wg_ref, wu_ref, wd_ref, o_ref, u_sc):
    @pl.when(pl.program_id(1) == 0)
    def _():
        x = x_ref[...]
        u_sc[...] = (_rms(x, x.shape[-1]) * g_ref[...]).astype(_BF16)
        o_ref[...] = x

    u = u_sc[...]
    gate = jnp.dot(u, wg_ref[...], preferred_element_type=_F32)
    up = jnp.dot(u, wu_ref[...], preferred_element_type=_F32)
    act = (gate * jax.nn.sigmoid(gate) * up).astype(_BF16)
    o_ref[...] += jnp.dot(act, wd_ref[...], preferred_element_type=_F32)


def _ffn(h, g, wg, wu, wd, *, tm, tf):
    t, d = h.shape
    f = wg.shape[1]
    return pl.pallas_call(
        _ffn_kernel,
        grid=(t // tm, f // tf),
        in_specs=[pl.BlockSpec((tm, d), lambda i, j: (i, 0)),
                  pl.BlockSpec((1, d), lambda i, j: (0, 0)),
                  pl.BlockSpec((d, tf), lambda i, j: (0, j)),
                  pl.BlockSpec((d, tf), lambda i, j: (0, j)),
                  pl.BlockSpec((tf, d), lambda i, j: (j, 0))],
        out_specs=pl.BlockSpec((tm, d), lambda i, j: (i, 0)),
        out_shape=jax.ShapeDtypeStruct((t, d), _F32),
        scratch_shapes=[pltpu.VMEM((tm, d), _BF16)],
        compiler_params=_cparams(2),
        name="ffn",
    )(h, g, wg, wu, wd)


def _ple_kernel(h_ref, p_ref, g_ref, wg_ref, wp_ref, o_ref):
    h = h_ref[...]
    u = (_rms(h, h.shape[-1]) * g_ref[...]).astype(_BF16)
    gate = jax.nn.sigmoid(jnp.dot(u, wg_ref[...], preferred_element_type=_F32))
    o_ref[...] = h + gate * jnp.dot(p_ref[...].astype(_BF16), wp_ref[...], preferred_element_type=_F32)


def _ple(h, p, g, wg, wp, *, tm):
    t, d = h.shape
    row = lambda i: (i, 0)
    const = lambda a: pl.BlockSpec(a.shape, (lambda i: (0, 0)), pipeline_mode=pl.Buffered(1))
    return pl.pallas_call(
        _ple_kernel,
        grid=(t // tm,),
        in_specs=[pl.BlockSpec((tm, d), row), pl.BlockSpec((tm, p.shape[1]), row),
                  const(g), const(wg), const(wp)],
        out_specs=pl.BlockSpec((tm, d), row),
        out_shape=jax.ShapeDtypeStruct((t, d), _F32),
        compiler_params=_cparams(1),
        name="ple",
    )(h, p, g, wg, wp)


def _pad_lanes(a, left=0, width=_LANE):
    pad = [(0, 0)] * (a.ndim - 1) + [(left, width - left - a.shape[-1])]
    return jnp.pad(a, pad)


def _rope_tables(pos, half, reps):
    inv_freq = jnp.exp(-math.log(_ROPE_THETA) * jnp.arange(half, dtype=_F32) / half)
    ang = pos.astype(_F32)[:, None] * inv_freq[None, :]
    cos, sin, zero = jnp.cos(ang), jnp.sin(ang), jnp.zeros_like(ang)
    tile = lambda a, b: _pad_lanes(jnp.concatenate([a, b] * reps, axis=-1))
    return tile(cos, cos), tile(zero, sin), tile(-sin, zero)


def _tile_of(n, want):
    t = min(n, want)
    while n % t:
        t //= 2
    return t


def kernel(x_prompt, x_sample, cache_mla_ckv, cache_mla_kpe, cache_fox_k, cache_fox_v, cache_fox_logf, cache_diff_k, cache_diff_v, page_table, p_prompt, p_sample, g_attn, w_in, g_mla_q, g_ckv, g_kpe, w_uk, w_uv, g_fox_q, g_fox_k, b_fox, g_diff_q, g_diff_k, lam_q1, lam_k1, lam_q2, lam_k2, g_diff_sub, w_out, g_ffn, w_gate, w_up, w_down, g_ple, w_ple_gate, w_ple_proj):
    batch, seq, d = x_prompt.shape
    dec_batch, n_new, _ = x_sample.shape
    depth = w_in.shape[0]
    page = cache_mla_ckv.shape[2]
    kv_rank = cache_mla_ckv.shape[3]
    mla_rope = cache_mla_kpe.shape[3]
    hm, nope = w_uk.shape[2], w_uk.shape[3]
    mla_v = w_uv.shape[3]
    mla_qk = nope + mla_rope
    fox_dim = cache_fox_k.shape[3]
    hf = cache_fox_logf.shape[3]
    dqk = cache_diff_k.shape[4]
    dv = cache_diff_v.shape[3]
    hd = (w_in.shape[2] - (hm * mla_qk + kv_rank + mla_rope + hf * fox_dim + 2 * fox_dim + hf
                           + 2 * dqk + dv)) // (2 * dqk)
    n_pages = page_table.shape[1]
    past_len = n_pages * page
    assert page == kv_rank == nope == mla_v == fox_dim == dv == 2 * dqk == _LANE
    assert n_new == 8 and mla_rope % 2 == 0 and mla_rope <= _LANE and hf <= 8

    t_p, t_s = batch * seq, dec_batch * n_new
    t = t_p + t_s
    tm = _tile_of(math.gcd(t_p, t_s), 256)
    tb = _tile_of(seq, 256)
    cp = _tile_of(n_pages, 16)
    tf = _tile_of(w_gate.shape[2], 512)
    tm_ffn = _tile_of(math.gcd(t_p, t_s), 512)

    pos = jnp.concatenate([jnp.tile(jnp.arange(seq, dtype=jnp.int32), batch),
                           jnp.tile(past_len + jnp.arange(n_new, dtype=jnp.int32), dec_batch)])
    tabs = jnp.stack(_rope_tables(pos, mla_rope // 2, 1) + _rope_tables(pos, dqk // 2, _LANE // dqk))

    kpe_t = jnp.transpose(cache_mla_kpe, (0, 1, 3, 2))
    logf_t = jnp.transpose(cache_fox_logf, (0, 1, 3, 2))
    dk_t = jnp.transpose(cache_diff_k, (0, 1, 3, 4, 2)).reshape(cache_diff_k.shape[:2] + (2 * dqk, page))

    h = jnp.concatenate([x_prompt.reshape(t_p, d), x_sample.reshape(t_s, d)], axis=0)
    offs = [0]
    for n in (hm * mla_qk, kv_rank, mla_rope, hf * fox_dim, fox_dim, fox_dim, hf,
              hd * 2 * dqk, 2 * dqk, dv):
        offs.append(offs[-1] + n)
    rows = [[] for _ in range(7)]
    for i in range(depth):
        w = w_in[i]
        col = lambda j: w[:, offs[j]:offs[j + 1]]
        wq = col(0).reshape(d, hm, mla_qk)
        wq = jnp.concatenate([wq[..., :nope], _pad_lanes(wq[..., nope:])], axis=-1).reshape(d, hm * 2 * _LANE)
        wm = jnp.concatenate([wq, col(1), _pad_lanes(col(2))], axis=-1).astype(_BF16)
        wf = jnp.concatenate([_pad_lanes(col(6)), col(3), col(4), col(5)], axis=-1).astype(_BF16)
        wdq = col(7).reshape(d, hd, 2, dqk)
        wdq = jnp.concatenate([_pad_lanes(wdq[:, :, 0]), _pad_lanes(wdq[:, :, 1], left=dqk)], axis=1)
        wd = jnp.concatenate([wdq.reshape(d, 2 * hd * _LANE), col(8), col(9)], axis=-1).astype(_BF16)
        wuk = jnp.transpose(w_uk[i], (1, 2, 0)).astype(_BF16)
        wuv = jnp.transpose(w_uv[i], (1, 0, 2)).astype(_BF16)
        gv = jnp.stack([g_mla_q[i][:nope], _pad_lanes(g_mla_q[i][nope:]), g_ckv[i], _pad_lanes(g_kpe[i]),
                        _pad_lanes(b_fox[i]), g_fox_q[i], g_fox_k[i],
                        jnp.tile(g_diff_q[i], 2), jnp.tile(g_diff_k[i], 2)]
                       + [jnp.zeros((_LANE,), _F32)] * 7)
        lamv = jnp.stack([_pad_lanes(v) for v in (lam_q1[i], lam_k1[i], lam_q2[i], lam_k2[i])]
                         + [jnp.zeros((_LANE,), _F32)] * 4)
        lam_init = 0.8 - 0.6 * math.exp(-0.3 * i)

        qm, qf, qd, c_new, kpe_new, kf_new, vf_new, logf_new, kd_new, vd_new = _inproj(
            h, g_attn[i][None], wm, wf, wd, wuk, gv, tabs, tm=tm, hm=hm, hf=hf, hd=hd,
            mla_qk=mla_qk, mla_rope=mla_rope, dqk=dqk)

        nb = seq // tb
        fc_col, fc_row = _prompt_cumsum(logf_new, batch=batch, seq=seq, blk=_LANE)
        pa = functools.partial(_prompt_attn, batch=batch, seq=seq, tb=tb)
        om_p = pa(functools.partial(_pa_mla_kernel, scale=mla_qk ** -0.5), qm, [c_new, kpe_new], [], g_out=hm)
        of_p = pa(functools.partial(_pa_fox_kernel, scale=fox_dim ** -0.5), qf, [kf_new, vf_new],
                  [(fc_col, pl.BlockSpec((tb, _LANE), lambda b, qi, ki: (b * nb + qi, 0))),
                   (fc_row, pl.BlockSpec((1, 8, tb), lambda b, qi, ki: (b, 0, jnp.minimum(ki, qi))))],
                  g_out=hf)
        lam_spec3 = pl.BlockSpec((8, _LANE), lambda b, qi, ki: (0, 0))
        od_p = pa(functools.partial(_pa_diff_kernel, scale=dqk ** -0.5, lam_init=lam_init), qd,
                  [kd_new, vd_new], [(lamv, lam_spec3)], g_out=hd)

        da = functools.partial(_decode_attn, tok0=t_p, n_new=n_new, cp=cp)
        om_s = da(functools.partial(_dec_mla_kernel, layer=i, cp=cp, n_pages=n_pages,
                                    scale=mla_qk ** -0.5, rope=mla_rope),
                  page_table, qm, [c_new, kpe_new], [], [cache_mla_ckv, kpe_t], [], g_out=hm)
        of_s = da(functools.partial(_dec_fox_kernel, layer=i, cp=cp, n_pages=n_pages, scale=fox_dim ** -0.5),
                  page_table, qf, [kf_new, vf_new, logf_new], [], [cache_fox_k, cache_fox_v, logf_t],
                  [pltpu.VMEM((hf, 1), _F32), pltpu.VMEM((n_new, _LANE), _F32)], g_out=hf)
        lam_spec2 = pl.BlockSpec((8, _LANE), lambda b, c, pt: (0, 0))
        od_s = da(functools.partial(_dec_diff_kernel, layer=i, cp=cp, n_pages=n_pages,
                                    scale=dqk ** -0.5, lam_init=lam_init),
                  page_table, qd, [kd_new, vd_new], [(lamv, lam_spec2)], [dk_t, cache_diff_v], [], g_out=hd)

        om = jnp.concatenate([om_p, om_s], axis=1)
        of = jnp.concatenate([of_p, of_s], axis=1)
        od = jnp.concatenate([od_p, od_s], axis=1)
        h = _merge(h, om, of, od, wuv, g_diff_sub[i][None], w_out[i].astype(_BF16), tm=tm,
                   sub_scale=1.0 - lam_init)
        h = _ffn(h, g_ffn[i][None], w_gate[i].astype(_BF16), w_up[i].astype(_BF16),
                 w_down[i].astype(_BF16), tm=tm_ffn, tf=tf)
        p = jnp.concatenate([p_prompt[i].reshape(t_p, -1), p_sample[i].reshape(t_s, -1)], axis=0)
        h = _ple(h, p, g_ple[i][None], w_ple_gate[i].astype(_BF16), w_ple_proj[i].astype(_BF16), tm=tm)

        for j, r in enumerate((c_new, kpe_new[:, :mla_rope], kf_new, vf_new, logf_new[:, :hf], kd_new, vd_new)):
            rows[j].append(r)

    outs = [h[:t_p].reshape(batch, seq, d), h[t_p:].reshape(dec_batch, n_new, d)]
    for j, rs in enumerate(rows):
        st = jnp.stack(rs)
        tail = (2, dqk) if j == 5 else (st.shape[-1],)
        outs.append(st[:, :t_p].reshape((depth, batch, seq) + tail))
        outs.append(st[:, t_p:].reshape((depth, dec_batch, n_new) + tail))
    return tuple(outs)
```

```python
import functools
import math

import jax
import jax.numpy as jnp
from jax import lax
from jax.experimental import pallas as pl
from jax.experimental.pallas import tpu as pltpu

_F32 = jnp.float32
_BF16 = jnp.bfloat16
_LANE = 128
_VMEM_LIMIT = 52 * 1024 * 1024
_NORM_EPS = 1e-6
_ROPE_THETA = 10000.0
_NEG_INF = float("-inf")
_PROMPT_Q_TILE = 128
_PROMPT_K_TILE = 512
_PAGES_PER_STEP = 64


def _cparams(n_grid):
    return pltpu.CompilerParams(dimension_semantics=("arbitrary",) * n_grid,
                                vmem_limit_bytes=_VMEM_LIMIT)


def _rms(x, n):
    ss = jnp.sum(x * x, axis=-1, keepdims=True)
    return x * lax.rsqrt(ss * (1.0 / n) + _NORM_EPS)


def _rope(x, cos, sin_a, sin_b, half):
    lanes = x.shape[-1]
    return x * cos + pltpu.roll(x, half, 1) * sin_a + pltpu.roll(x, lanes - half, 1) * sin_b


def _split3(x):
    hi = x.astype(_BF16)
    r1 = x - hi.astype(_F32)
    mid = r1.astype(_BF16)
    lo = (r1 - mid.astype(_F32)).astype(_BF16)
    return hi, mid, lo


def _tri(n, kind):
    r = lax.broadcasted_iota(jnp.int32, (n, n), 0)
    c = lax.broadcasted_iota(jnp.int32, (n, n), 1)
    if kind == "lower_incl":
        m = c <= r
    else:
        m = r > c
    return jnp.where(m, 1.0, 0.0).astype(_BF16)


def _online_update(s, v_bf, m_sc, l_sc, acc_sc):
    m_prev = m_sc[...]
    m_new = jnp.maximum(m_prev, jnp.max(s, axis=-1, keepdims=True))
    alpha = jnp.exp(m_prev - m_new)
    p = jnp.exp(s - m_new)
    l_sc[...] = alpha * l_sc[...] + jnp.sum(p, axis=-1, keepdims=True)
    acc_sc[...] = alpha * acc_sc[...] + jnp.dot(p.astype(_BF16), v_bf, preferred_element_type=_F32)
    m_sc[...] = m_new


def _nt_dot(a, b):
    return lax.dot_general(a, b, (((1,), (1,)), ((), ())), preferred_element_type=_F32)


def _inproj_kernel(x_ref, gattn_ref, wm_ref, wf_ref, wd_ref, wuk_ref, gv_ref, tab_ref,
                   qm_ref, qf_ref, qd_ref, c_ref, kpe_ref, kf_ref, vf_ref, logf_ref, kd_ref, vd_ref,
                   *, hm, hf, hd, mla_qk, mla_rope, dqk):
    x = x_ref[...]
    u = (_rms(x, x.shape[-1]) * gattn_ref[...]).astype(_BF16)
    g_nope, g_rope, g_ckv, g_kpe = gv_ref[0:1], gv_ref[1:2], gv_ref[2:3], gv_ref[3:4]
    b_fox, g_fq, g_fk, g_dq, g_dk = gv_ref[4:5], gv_ref[5:6], gv_ref[6:7], gv_ref[7:8], gv_ref[8:9]
    cos_m, sin_ma, sin_mb = tab_ref[0], tab_ref[1], tab_ref[2]
    cos_d, sin_da, sin_db = tab_ref[3], tab_ref[4], tab_ref[5]
    half_m, half_d = mla_rope // 2, dqk // 2

    pm = _nt_dot(u, wm_ref[...])
    for h in range(hm):
        nope = pm[:, 2 * _LANE * h:2 * _LANE * h + _LANE]
        rp = pm[:, 2 * _LANE * h + _LANE:2 * _LANE * (h + 1)]
        ss = jnp.sum(nope * nope, axis=-1, keepdims=True) + jnp.sum(rp * rp, axis=-1, keepdims=True)
        r = lax.rsqrt(ss * (1.0 / mla_qk) + _NORM_EPS)
        nn = nope * r * g_nope
        rr = _rope(rp * r * g_rope, cos_m, sin_ma, sin_mb, half_m)
        qm_ref[h, :, 0:_LANE] = jnp.dot(nn.astype(_BF16), wuk_ref[h], preferred_element_type=_F32)
        qm_ref[h, :, _LANE:2 * _LANE] = rr
    base = 2 * _LANE * hm
    c_ref[...] = _rms(pm[:, base:base + _LANE], _LANE) * g_ckv
    kr = _rms(pm[:, base + _LANE:base + 2 * _LANE], mla_rope) * g_kpe
    kpe_ref[...] = _rope(kr, cos_m, sin_ma, sin_mb, half_m)

    pf = _nt_dot(u, wf_ref[...])
    z = pf[:, 0:_LANE] + b_fox
    logf_ref[...] = jnp.minimum(z, 0.0) - jnp.log1p(jnp.exp(-jnp.abs(z)))
    for h in range(hf):
        qf_ref[h] = _rms(pf[:, _LANE * (1 + h):_LANE * (2 + h)], _LANE) * g_fq
    base = _LANE * (1 + hf)
    kf_ref[...] = _rms(pf[:, base:base + _LANE], _LANE) * g_fk
    vf_ref[...] = pf[:, base + _LANE:base + 2 * _LANE]

    pd = _nt_dot(u, wd_ref[...])
    for j in range(2 * hd):
        qn = _rms(pd[:, _LANE * j:_LANE * (j + 1)], dqk) * g_dq
        qd_ref[j] = _rope(qn, cos_d, sin_da, sin_db, half_d)
    base = _LANE * 2 * hd
    k = pd[:, base:base + _LANE]
    first = lax.broadcasted_iota(jnp.int32, k.shape, 1) < dqk
    k2 = k * k
    ss0 = jnp.sum(jnp.where(first, k2, 0.0), axis=-1, keepdims=True)
    ss1 = jnp.sum(jnp.where(first, 0.0, k2), axis=-1, keepdims=True)
    r = lax.rsqrt(jnp.where(first, ss0, ss1) * (1.0 / dqk) + _NORM_EPS)
    kd_ref[...] = _rope(k * r * g_dk, cos_d, sin_da, sin_db, half_d)
    vd_ref[...] = pd[:, base + _LANE:base + 2 * _LANE]


def _inproj(x, gattn, wm, wf, wd, wuk, gv, tabs, *, tm, hm, hf, hd, mla_qk, mla_rope, dqk):
    t, d = x.shape
    const2 = lambda i: (0, 0)
    const3 = lambda i: (0, 0, 0)
    row = lambda i: (i, 0)
    hrow = lambda i: (0, i, 0)
    w_spec = lambda w: pl.BlockSpec(w.shape, const2 if w.ndim == 2 else const3,
                                    pipeline_mode=pl.Buffered(1))
    tok = pl.BlockSpec((tm, _LANE), row)
    tok_shape = jax.ShapeDtypeStruct((t, _LANE), _F32)
    kern = functools.partial(_inproj_kernel, hm=hm, hf=hf, hd=hd, mla_qk=mla_qk,
                             mla_rope=mla_rope, dqk=dqk)
    return pl.pallas_call(
        kern,
        grid=(t // tm,),
        in_specs=[pl.BlockSpec((tm, d), row), w_spec(gattn), w_spec(wm), w_spec(wf), w_spec(wd),
                  w_spec(wuk), w_spec(gv), pl.BlockSpec((tabs.shape[0], tm, _LANE), hrow)],
        out_specs=[pl.BlockSpec((hm, tm, 2 * _LANE), hrow), pl.BlockSpec((hf, tm, _LANE), hrow),
                   pl.BlockSpec((2 * hd, tm, _LANE), hrow)] + [tok] * 7,
        out_shape=[jax.ShapeDtypeStruct((hm, t, 2 * _LANE), _F32),
                   jax.ShapeDtypeStruct((hf, t, _LANE), _F32),
                   jax.ShapeDtypeStruct((2 * hd, t, _LANE), _F32)] + [tok_shape] * 7,
        compiler_params=_cparams(1),
        name="inproj",
    )(x, gattn, wm, wf, wd, wuk, gv, tabs)


def _cumsum_kernel(x_ref, col_ref, row_ref, carry_sc):
    @pl.when(pl.program_id(1) == 0)
    def _():
        carry_sc[...] = jnp.zeros_like(carry_sc)

    n = x_ref.shape[0]
    tri = _tri(n, "lower_incl")
    cum = carry_sc[...] + sum(jnp.dot(tri, p, preferred_element_type=_F32) for p in _split3(x_ref[...]))
    col_ref[...] = cum
    carry_sc[...] = cum[n - 1:n, :]
    row_ref[0] = cum.T[0:row_ref.shape[1], :]


def _prompt_cumsum(logf, *, batch, seq, blk):
    nb = seq // blk
    return pl.pallas_call(
        _cumsum_kernel,
        grid=(batch, nb),
        in_specs=[pl.BlockSpec((blk, _LANE), lambda b, i: (b * nb + i, 0))],
        out_specs=[pl.BlockSpec((blk, _LANE), lambda b, i: (b * nb + i, 0)),
                   pl.BlockSpec((1, 8, blk), lambda b, i: (b, 0, i))],
        out_shape=[jax.ShapeDtypeStruct((batch * seq, _LANE), _F32),
                   jax.ShapeDtypeStruct((batch, 8, seq), _F32)],
        scratch_shapes=[pltpu.VMEM((1, _LANE), _F32)],
        compiler_params=_cparams(2),
        name="prompt_cumsum",
    )(logf)


def _causal_mask(s, g, tq, tk, qi, ki):
    s3 = s.reshape(g, tq, tk)
    row = qi * tq + lax.broadcasted_iota(jnp.int32, (1, tq, tk), 1)
    col = ki * tk + lax.broadcasted_iota(jnp.int32, (1, tq, tk), 2)
    return jnp.where(col <= row, s3, _NEG_INF).reshape(g * tq, tk)


def _pa_flash(qt_ref, kt_ref, q_ref, tk, m_sc, l_sc, acc_sc, scores, values, finalize):
    step_id = pl.program_id(1)
    qi, ki = qt_ref[step_id], kt_ref[step_id]
    g, tq, dq = q_ref.shape
    last = (qi * tq + tq - 1) // tk

    @pl.when(ki == 0)
    def _():
        _flash_init(m_sc, l_sc, acc_sc)

    def step(diag):
        s = scores(q_ref[...].reshape(g * tq, dq).astype(_BF16))
        if diag:
            s = _causal_mask(s, g, tq, tk, qi, ki)
        _online_update(s, values(), m_sc, l_sc, acc_sc)

    @pl.when(ki < last)
    def _():
        step(False)

    @pl.when(ki == last)
    def _():
        step(True)
        finalize()


def _flash_init(m_sc, l_sc, acc_sc):
    m_sc[...] = jnp.full_like(m_sc, _NEG_INF)
    l_sc[...] = jnp.zeros_like(l_sc)
    acc_sc[...] = jnp.zeros_like(acc_sc)


def _pa_mla_kernel(qt_ref, kt_ref, q_ref, c_ref, kpe_ref, o_ref, m_sc, l_sc, acc_sc, *, scale):
    def scores(q):
        k = jnp.concatenate([c_ref[...], kpe_ref[...]], axis=-1).astype(_BF16)
        return _nt_dot(q, k) * scale

    def finalize():
        o_ref[...] = (acc_sc[...] / l_sc[...]).reshape(o_ref.shape)

    _pa_flash(qt_ref, kt_ref, q_ref, c_ref.shape[0], m_sc, l_sc, acc_sc, scores,
              lambda: c_ref[...].astype(_BF16), finalize)


def _pa_fox_kernel(qt_ref, kt_ref, q_ref, k_ref, v_ref, fq_ref, fk_ref, o_ref, m_sc, l_sc, acc_sc,
                   *, scale):
    g, tq, _ = q_ref.shape

    def scores(q):
        s = _nt_dot(q, k_ref[...].astype(_BF16)) * scale
        fq = fq_ref[...]
        fk = fk_ref[0]
        return jnp.concatenate(
            [s[h * tq:(h + 1) * tq] + (fq[:, h:h + 1] - fk[h:h + 1, :]) for h in range(g)], axis=0)

    def finalize():
        o_ref[...] = (acc_sc[...] / l_sc[...]).reshape(o_ref.shape)

    _pa_flash(qt_ref, kt_ref, q_ref, k_ref.shape[0], m_sc, l_sc, acc_sc, scores,
              lambda: v_ref[...].astype(_BF16), finalize)


def _diff_lambda(lam_ref, lam_init):
    a = jnp.sum(lam_ref[0:1] * lam_ref[1:2], axis=-1, keepdims=True)
    b = jnp.sum(lam_ref[2:3] * lam_ref[3:4], axis=-1, keepdims=True)
    return jnp.exp(a) - jnp.exp(b) + lam_init


def _diff_combine(acc, l, lam, half):
    w = acc / l
    return w[0:half] - lam * w[half:2 * half]


def _pa_diff_kernel(qt_ref, kt_ref, q_ref, k_ref, v_ref, lam_ref, o_ref, m_sc, l_sc, acc_sc,
                    *, scale, lam_init):
    g, tq, _ = q_ref.shape

    def finalize():
        lam = _diff_lambda(lam_ref, lam_init)
        o_ref[...] = _diff_combine(acc_sc[...], l_sc[...], lam, (g // 2) * tq).reshape(o_ref.shape)

    _pa_flash(qt_ref, kt_ref, q_ref, k_ref.shape[0], m_sc, l_sc, acc_sc,
              lambda q: _nt_dot(q, k_ref[...].astype(_BF16)) * scale,
              lambda: v_ref[...].astype(_BF16), finalize)


def _prompt_attn(kern, q, kv_list, extra, *, batch, seq, tq, tk, g_out):
    g, _, dq = q.shape
    nq, nk = seq // tq, seq // tk
    pairs = [(qi, ki) for qi in range(nq) for ki in range((qi * tq + tq - 1) // tk + 1)]
    qt = jnp.asarray([p[0] for p in pairs], jnp.int32)
    kt = jnp.asarray([p[1] for p in pairs], jnp.int32)

    def spec(block, index):
        return pl.BlockSpec(block, lambda b, s, qt_ref, kt_ref: index(b, qt_ref[s], kt_ref[s]))

    grid_spec = pltpu.PrefetchScalarGridSpec(
        num_scalar_prefetch=2,
        grid=(batch, len(pairs)),
        in_specs=[spec((g, tq, dq), lambda b, qi, ki: (0, b * nq + qi, 0))]
                 + [spec((tk, _LANE), lambda b, qi, ki: (b * nk + ki, 0))] * len(kv_list)
                 + [spec(block, index) for _, block, index in extra],
        out_specs=spec((g_out, tq, _LANE), lambda b, qi, ki: (0, b * nq + qi, 0)),
        scratch_shapes=[pltpu.VMEM((g * tq, 1), _F32), pltpu.VMEM((g * tq, 1), _F32),
                        pltpu.VMEM((g * tq, _LANE), _F32)],
    )
    return pl.pallas_call(
        kern,
        grid_spec=grid_spec,
        out_shape=jax.ShapeDtypeStruct((g_out, batch * seq, _LANE), _F32),
        compiler_params=_cparams(2),
        name=kern.func.__name__.strip("_"),
    )(qt, kt, q, *kv_list, *[a for a, _, _ in extra])


def _page_copies(pt_ref, chunk, slot, layer, cp, reverse, n_pages, hbm_refs, bufs, sems):
    copies = []
    chunks_per_seq = n_pages // cp
    for i in range(cp):
        if reverse:
            b = chunk // chunks_per_seq
            c = chunk % chunks_per_seq
            idx = b * n_pages + (n_pages - 1 - (c * cp + i))
        else:
            idx = chunk * cp + i
        page = pt_ref[idx]
        for a, (hbm, buf) in enumerate(zip(hbm_refs, bufs)):
            copies.append(pltpu.make_async_copy(hbm.at[layer, page], buf.at[slot, i], sems.at[slot, a]))
    return copies


def _decode_pipeline(pt_ref, layer, cp, reverse, n_pages, hbm_refs, bufs, sems):
    nc = pl.num_programs(1)
    g = pl.program_id(0) * nc + pl.program_id(1)
    total = pl.num_programs(0) * nc
    slot = g % 2
    args = (layer, cp, reverse, n_pages, hbm_refs, bufs, sems)

    @pl.when(g == 0)
    def _():
        for cpy in _page_copies(pt_ref, g, slot, *args):
            cpy.start()

    @pl.when(g + 1 < total)
    def _():
        for cpy in _page_copies(pt_ref, g + 1, 1 - slot, *args):
            cpy.start()

    for cpy in _page_copies(pt_ref, g, slot, *args):
        cpy.wait()
    return slot


def _new_token_mask(s, n_new):
    t = lax.broadcasted_iota(jnp.int32, s.shape, 0) % n_new
    col = lax.broadcasted_iota(jnp.int32, s.shape, 1)
    return jnp.where(col <= t, s, _NEG_INF)


def _pad_rows(x, rows):
    return jnp.concatenate([x, jnp.zeros((rows - x.shape[0], x.shape[1]), x.dtype)], axis=0)


def _dec_mla_kernel(pt_ref, q_ref, cn_ref, kn_ref, ckv_hbm, kpe_hbm, o_ref,
                    ckv_buf, kpe_buf, sems, m_sc, l_sc, acc_sc, *, layer, cp, n_pages, scale, rope):
    c_idx = pl.program_id(1)
    g, n_new, dq = q_ref.shape
    slot = _decode_pipeline(pt_ref, layer, cp, False, n_pages, (ckv_hbm, kpe_hbm),
                            (ckv_buf, kpe_buf), sems)
    q = q_ref[...].reshape(g * n_new, dq)
    q_lat = q[:, 0:_LANE].astype(_BF16)
    q_pe = q[:, _LANE:_LANE + rope].astype(_BF16)

    @pl.when(c_idx == 0)
    def _():
        _flash_init(m_sc, l_sc, acc_sc)
        cn = _pad_rows(cn_ref[...], _LANE)
        kn = jnp.concatenate([cn, _pad_rows(kn_ref[...], _LANE)], axis=-1).astype(_BF16)
        s = _new_token_mask(_nt_dot(q.astype(_BF16), kn) * scale, n_new)
        _online_update(s, cn.astype(_BF16), m_sc, l_sc, acc_sc)

    c = ckv_buf[slot].reshape(cp * _LANE, _LANE).astype(_BF16)
    s_pe = jnp.concatenate(
        [jnp.dot(q_pe, kpe_buf[slot, i].astype(_BF16), preferred_element_type=_F32) for i in range(cp)],
        axis=-1)
    s = (_nt_dot(q_lat, c) + s_pe) * scale
    _online_update(s, c, m_sc, l_sc, acc_sc)

    @pl.when(c_idx == pl.num_programs(1) - 1)
    def _():
        o_ref[...] = (acc_sc[...] / l_sc[...]).reshape(o_ref.shape)


def _dec_fox_kernel(pt_ref, q_ref, kn_ref, vn_ref, fn_ref, k_hbm, v_hbm, f_hbm, o_ref,
                    k_buf, v_buf, f_buf, sems, m_sc, l_sc, acc_sc, carry_sc, ncum_sc,
                    *, layer, cp, n_pages, scale):
    c_idx = pl.program_id(1)
    g, n_new, dq = q_ref.shape
    slot = _decode_pipeline(pt_ref, layer, cp, True, n_pages, (k_hbm, v_hbm, f_hbm),
                            (k_buf, v_buf, f_buf), sems)
    q = q_ref[...].reshape(g * n_new, dq).astype(_BF16)

    def head_rows(per_head):
        return jnp.concatenate([jnp.broadcast_to(per_head(h), (n_new, per_head(h).shape[-1]))
                                for h in range(g)], axis=0)

    @pl.when(c_idx == 0)
    def _():
        _flash_init(m_sc, l_sc, acc_sc)
        carry_sc[...] = jnp.zeros_like(carry_sc)
        fn = _pad_rows(fn_ref[...], _LANE)
        ncol = sum(jnp.dot(_tri(_LANE, "lower_incl"), p, preferred_element_type=_F32)
                   for p in _split3(fn))
        ncum_sc[...] = ncol[0:n_new, :]
        nrow = ncol.T
        kn = _pad_rows(kn_ref[...], _LANE).astype(_BF16)
        s = _nt_dot(q, kn) * scale
        s = s + head_rows(lambda h: ncol[0:n_new, h:h + 1] - nrow[h:h + 1, :])
        s = _new_token_mask(s, n_new)
        _online_update(s, _pad_rows(vn_ref[...], _LANE).astype(_BF16), m_sc, l_sc, acc_sc)

    k = k_buf[slot].reshape(cp * _LANE, _LANE).astype(_BF16)
    s = _nt_dot(q, k) * scale
    sgt = _tri(_LANE, "row_gt_col")
    run = carry_sc[...]
    biases = []
    for i in range(cp):
        f = f_buf[slot, i]
        within = sum(jnp.dot(p, sgt, preferred_element_type=_F32) for p in _split3(f))
        biases.append(run + within)
        run = run + jnp.sum(f, axis=-1, keepdims=True)
    carry_sc[...] = run
    bias = jnp.concatenate(biases, axis=-1)
    ncum = ncum_sc[...]
    s = s + head_rows(lambda h: bias[h:h + 1, :]) + jnp.concatenate(
        [ncum[:, h:h + 1] for h in range(g)], axis=0)
    _online_update(s, v_buf[slot].reshape(cp * _LANE, _LANE).astype(_BF16), m_sc, l_sc, acc_sc)

    @pl.when(c_idx == pl.num_programs(1) - 1)
    def _():
        o_ref[...] = (acc_sc[...] / l_sc[...]).reshape(o_ref.shape)


def _dec_diff_kernel(pt_ref, q_ref, kn_ref, vn_ref, lam_ref, k_hbm, v_hbm, o_ref,
                     k_buf, v_buf, sems, m_sc, l_sc, acc_sc,
                     *, layer, cp, n_pages, scale, lam_init):
    c_idx = pl.program_id(1)
    g, n_new, dq = q_ref.shape
    slot = _decode_pipeline(pt_ref, layer, cp, False, n_pages, (k_hbm, v_hbm), (k_buf, v_buf), sems)
    q = q_ref[...].reshape(g * n_new, dq).astype(_BF16)

    @pl.when(c_idx == 0)
    def _():
        _flash_init(m_sc, l_sc, acc_sc)
        kn = _pad_rows(kn_ref[...], _LANE).astype(_BF16)
        s = _new_token_mask(_nt_dot(q, kn) * scale, n_new)
        _online_update(s, _pad_rows(vn_ref[...], _LANE).astype(_BF16), m_sc, l_sc, acc_sc)

    s = jnp.concatenate(
        [jnp.dot(q, k_buf[slot, i].astype(_BF16), preferred_element_type=_F32) for i in range(cp)],
        axis=-1) * scale
    _online_update(s, v_buf[slot].reshape(cp * _LANE, _LANE).astype(_BF16), m_sc, l_sc, acc_sc)

    @pl.when(c_idx == pl.num_programs(1) - 1)
    def _():
        lam = _diff_lambda(lam_ref, lam_init)
        o_ref[...] = _diff_combine(acc_sc[...], l_sc[...], lam, (g // 2) * n_new).reshape(o_ref.shape)


def _decode_attn(kern, page_table, q, new_rows, extra, caches, extra_scratch, *,
                 tok0, n_new, cp, g_out):
    g, _, dq = q.shape
    dec_batch, n_pages = page_table.shape
    blk0 = tok0 // n_new
    new_spec = pl.BlockSpec((n_new, _LANE), lambda b, c, pt: (blk0 + b, 0))
    grid_spec = pltpu.PrefetchScalarGridSpec(
        num_scalar_prefetch=1,
        grid=(dec_batch, n_pages // cp),
        in_specs=[pl.BlockSpec((g, n_new, dq), lambda b, c, pt: (0, blk0 + b, 0))]
                 + [new_spec] * len(new_rows) + [s for _, s in extra]
                 + [pl.BlockSpec(memory_space=pl.ANY)] * len(caches),
        out_specs=pl.BlockSpec((g_out, n_new, _LANE), lambda b, c, pt: (0, b, 0)),
        scratch_shapes=[pltpu.VMEM((2, cp) + c.shape[2:], _F32) for c in caches]
                       + [pltpu.SemaphoreType.DMA((2, len(caches))),
                          pltpu.VMEM((g * n_new, 1), _F32), pltpu.VMEM((g * n_new, 1), _F32),
                          pltpu.VMEM((g * n_new, _LANE), _F32)] + list(extra_scratch),
    )
    return pl.pallas_call(
        kern,
        grid_spec=grid_spec,
        out_shape=jax.ShapeDtypeStruct((g_out, dec_batch * n_new, _LANE), _F32),
        compiler_params=_cparams(2),
        name=kern.func.__name__.strip("_"),
    )(page_table.reshape(-1), q, *new_rows, *[a for a, _ in extra], *caches)


def _merge_kernel(h_ref, om_ref, of_ref, od_ref, wuv_ref, gsub_ref, wout_ref, o_ref, *, sub_scale):
    parts = []
    for h in range(om_ref.shape[0]):
        parts.append(jnp.dot(om_ref[h].astype(_BF16), wuv_ref[h],
                             preferred_element_type=_F32).astype(_BF16))
    for h in range(of_ref.shape[0]):
        parts.append(of_ref[h].astype(_BF16))
    for h in range(od_ref.shape[0]):
        parts.append((_rms(od_ref[h], _LANE) * gsub_ref[...] * sub_scale).astype(_BF16))
    mix = jnp.concatenate(parts, axis=-1)
    o_ref[...] = h_ref[...] + jnp.dot(mix, wout_ref[...], preferred_element_type=_F32)


def _merge(h, om, of, od, wuv, gsub, wout, *, tm, sub_scale):
    t, d = h.shape
    row = lambda i: (i, 0)
    hrow = lambda i: (0, i, 0)
    const = lambda a: pl.BlockSpec(a.shape, (lambda i: (0,) * a.ndim), pipeline_mode=pl.Buffered(1))
    return pl.pallas_call(
        functools.partial(_merge_kernel, sub_scale=sub_scale),
        grid=(t // tm,),
        in_specs=[pl.BlockSpec((tm, d), row)]
                 + [pl.BlockSpec((a.shape[0], tm, _LANE), hrow) for a in (om, of, od)]
                 + [const(wuv), const(gsub), const(wout)],
        out_specs=pl.BlockSpec((tm, d), row),
        out_shape=jax.ShapeDtypeStruct((t, d), _F32),
        compiler_params=_cparams(1),
        name="merge",
    )(h, om, of, od, wuv, gsub, wout)


def _ffn_kernel(x_ref, g_ref, wg_ref, wu_ref, wd_ref, o_ref, u_sc):
    @pl.when(pl.program_id(1) == 0)
    def _():
        x = x_ref[...]
        u_sc[...] = (_rms(x, x.shape[-1]) * g_ref[...]).astype(_BF16)
        o_ref[...] = x

    u = u_sc[...]
    gate = jnp.dot(u, wg_ref[...], preferred_element_type=_F32)
    up = jnp.dot(u, wu_ref[...], preferred_element_type=_F32)
    act = (gate * jax.nn.sigmoid(gate) * up).astype(_BF16)
    o_ref[...] += jnp.dot(act, wd_ref[...], preferred_element_type=_F32)


def _ffn(h, g, wg, wu, wd, *, tm, tf):
    t, d = h.shape
    f = wg.shape[1]
    return pl.pallas_call(
        _ffn_kernel,
        grid=(t // tm, f // tf),
        in_specs=[pl.BlockSpec((tm, d), lambda i, j: (i, 0)),
                  pl.BlockSpec((1, d), lambda i, j: (0, 0)),
                  pl.BlockSpec((d, tf), lambda i, j: (0, j)),
                  pl.BlockSpec((d, tf), lambda i, j: (0, j)),
                  pl.BlockSpec((tf, d), lambda i, j: (j, 0))],
        out_specs=pl.BlockSpec((tm, d), lambda i, j: (i, 0)),
        out_shape=jax.ShapeDtypeStruct((t, d), _F32),
        scratch_shapes=[pltpu.VMEM((tm, d), _BF16)],
        compiler_params=_cparams(2),
        name="ffn",
    )(h, g, wg, wu, wd)


def _ple_kernel(h_ref, p_ref, g_ref, wg_ref, wp_ref, o_ref):
    h = h_ref[...]
    u = (_rms(h, h.shape[-1]) * g_ref[...]).astype(_BF16)
    gate = jax.nn.sigmoid(jnp.dot(u, wg_ref[...], preferred_element_type=_F32))
    o_ref[...] = h + gate * jnp.dot(p_ref[...].astype(_BF16), wp_ref[...], preferred_element_type=_F32)


def _ple(h, p, g, wg, wp, *, tm):
    t, d = h.shape
    row = lambda i: (i, 0)
    const = lambda a: pl.BlockSpec(a.shape, (lambda i: (0, 0)), pipeline_mode=pl.Buffered(1))
    return pl.pallas_call(
        _ple_kernel,
        grid=(t // tm,),
        in_specs=[pl.BlockSpec((tm, d), row), pl.BlockSpec((tm, p.shape[1]), row),
                  const(g), const(wg), const(wp)],
        out_specs=pl.BlockSpec((tm, d), row),
        out_shape=jax.ShapeDtypeStruct((t, d), _F32),
        compiler_params=_cparams(1),
        name="ple",
    )(h, p, g, wg, wp)


def _pad_lanes(a, left=0, width=_LANE):
    pad = [(0, 0)] * (a.ndim - 1) + [(left, width - left - a.shape[-1])]
    return jnp.pad(a, pad)


def _pad_axis(a, axis, before=0, total=_LANE):
    pad = [(0, 0)] * a.ndim
    pad[axis] = (before, total - before - a.shape[axis])
    return jnp.pad(a, pad)


def _rope_tables(pos, half, reps):
    inv_freq = jnp.exp(-math.log(_ROPE_THETA) * jnp.arange(half, dtype=_F32) / half)
    ang = pos.astype(_F32)[:, None] * inv_freq[None, :]
    cos, sin, zero = jnp.cos(ang), jnp.sin(ang), jnp.zeros_like(ang)
    tile = lambda a, b: _pad_lanes(jnp.concatenate([a, b] * reps, axis=-1))
    return tile(cos, cos), tile(zero, sin), tile(-sin, zero)


def _tile_of(n, want):
    t = min(n, want)
    while n % t:
        t //= 2
    return t


def kernel(x_prompt, x_sample, cache_mla_ckv, cache_mla_kpe, cache_fox_k, cache_fox_v, cache_fox_logf, cache_diff_k, cache_diff_v, page_table, p_prompt, p_sample, g_attn, w_in, g_mla_q, g_ckv, g_kpe, w_uk, w_uv, g_fox_q, g_fox_k, b_fox, g_diff_q, g_diff_k, lam_q1, lam_k1, lam_q2, lam_k2, g_diff_sub, w_out, g_ffn, w_gate, w_up, w_down, g_ple, w_ple_gate, w_ple_proj):
    batch, seq, d = x_prompt.shape
    dec_batch, n_new, _ = x_sample.shape
    depth = w_in.shape[0]
    page = cache_mla_ckv.shape[2]
    kv_rank = cache_mla_ckv.shape[3]
    mla_rope = cache_mla_kpe.shape[3]
    hm, nope = w_uk.shape[2], w_uk.shape[3]
    mla_v = w_uv.shape[3]
    mla_qk = nope + mla_rope
    fox_dim = cache_fox_k.shape[3]
    hf = cache_fox_logf.shape[3]
    dqk = cache_diff_k.shape[4]
    dv = cache_diff_v.shape[3]
    hd = (w_in.shape[2] - (hm * mla_qk + kv_rank + mla_rope + hf * fox_dim + 2 * fox_dim + hf
                           + 2 * dqk + dv)) // (2 * dqk)
    n_pages = page_table.shape[1]
    past_len = n_pages * page
    assert page == kv_rank == nope == mla_v == fox_dim == dv == 2 * dqk == _LANE
    assert n_new == 8 and mla_rope % 2 == 0 and mla_rope <= _LANE and hf <= 8

    t_p, t_s = batch * seq, dec_batch * n_new
    t = t_p + t_s
    tm = _tile_of(math.gcd(t_p, t_s), 256)
    tq = _tile_of(seq, _PROMPT_Q_TILE)
    tk = _tile_of(seq, _PROMPT_K_TILE)
    cp = _tile_of(n_pages, _PAGES_PER_STEP)
    tf = _tile_of(w_gate.shape[2], 512)
    tm_ffn = _tile_of(math.gcd(t_p, t_s), 512)

    pos = jnp.concatenate([jnp.tile(jnp.arange(seq, dtype=jnp.int32), batch),
                           jnp.tile(past_len + jnp.arange(n_new, dtype=jnp.int32), dec_batch)])
    tabs = jnp.stack(_rope_tables(pos, mla_rope // 2, 1) + _rope_tables(pos, dqk // 2, _LANE // dqk))

    kpe_t = jnp.transpose(cache_mla_kpe, (0, 1, 3, 2))
    logf_t = jnp.transpose(cache_fox_logf, (0, 1, 3, 2))
    dk_t = jnp.transpose(cache_diff_k, (0, 1, 3, 4, 2)).reshape(cache_diff_k.shape[:2] + (2 * dqk, page))

    w_in_t = jnp.transpose(w_in, (2, 0, 1))
    h = jnp.concatenate([x_prompt.reshape(t_p, d), x_sample.reshape(t_s, d)], axis=0)
    offs = [0]
    for n in (hm * mla_qk, kv_rank, mla_rope, hf * fox_dim, fox_dim, fox_dim, hf,
              hd * 2 * dqk, 2 * dqk, dv):
        offs.append(offs[-1] + n)
    rows = [[] for _ in range(7)]
    for i in range(depth):
        w = w_in_t[:, i, :]
        blk = lambda j: w[offs[j]:offs[j + 1]]
        wq = blk(0).reshape(hm, mla_qk, d)
        wq = jnp.concatenate([wq[:, :nope], _pad_axis(wq[:, nope:], 1)], axis=1).reshape(hm * 2 * _LANE, d)
        wm = jnp.concatenate([wq, blk(1), _pad_axis(blk(2), 0)], axis=0).astype(_BF16)
        wf = jnp.concatenate([_pad_axis(blk(6), 0), blk(3), blk(4), blk(5)], axis=0).astype(_BF16)
        wdq = blk(7).reshape(hd, 2, dqk, d)
        wdq = jnp.concatenate([_pad_axis(wdq[:, 0], 1), _pad_axis(wdq[:, 1], 1, before=dqk)], axis=0)
        wd = jnp.concatenate([wdq.reshape(2 * hd * _LANE, d), blk(8), blk(9)], axis=0).astype(_BF16)
        wuk = jnp.transpose(w_uk[i], (1, 2, 0)).astype(_BF16)
        wuv = jnp.transpose(w_uv[i], (1, 0, 2)).astype(_BF16)
        gv = jnp.stack([g_mla_q[i][:nope], _pad_lanes(g_mla_q[i][nope:]), g_ckv[i], _pad_lanes(g_kpe[i]),
                        _pad_lanes(b_fox[i]), g_fox_q[i], g_fox_k[i],
                        jnp.tile(g_diff_q[i], 2), jnp.tile(g_diff_k[i], 2)]
                       + [jnp.zeros((_LANE,), _F32)] * 7)
        lamv = jnp.stack([_pad_lanes(v) for v in (lam_q1[i], lam_k1[i], lam_q2[i], lam_k2[i])]
                         + [jnp.zeros((_LANE,), _F32)] * 4)
        lam_init = 0.8 - 0.6 * math.exp(-0.3 * i)

        qm, qf, qd, c_new, kpe_new, kf_new, vf_new, logf_new, kd_new, vd_new = _inproj(
            h, g_attn[i][None], wm, wf, wd, wuk, gv, tabs, tm=tm, hm=hm, hf=hf, hd=hd,
            mla_qk=mla_qk, mla_rope=mla_rope, dqk=dqk)

        nq = seq // tq
        fc_col, fc_row = _prompt_cumsum(logf_new, batch=batch, seq=seq, blk=_LANE)
        pa = functools.partial(_prompt_attn, batch=batch, seq=seq, tq=tq, tk=tk)
        om_p = pa(functools.partial(_pa_mla_kernel, scale=mla_qk ** -0.5), qm, [c_new, kpe_new], [], g_out=hm)
        of_p = pa(functools.partial(_pa_fox_kernel, scale=fox_dim ** -0.5), qf, [kf_new, vf_new],
                  [(fc_col, (tq, _LANE), lambda b, qi, ki: (b * nq + qi, 0)),
                   (fc_row, (1, 8, tk), lambda b, qi, ki: (b, 0, ki))],
                  g_out=hf)
        od_p = pa(functools.partial(_pa_diff_kernel, scale=dqk ** -0.5, lam_init=lam_init), qd,
                  [kd_new, vd_new], [(lamv, (8, _LANE), lambda b, qi, ki: (0, 0))], g_out=hd)

        da = functools.partial(_decode_attn, tok0=t_p, n_new=n_new, cp=cp)
        om_s = da(functools.partial(_dec_mla_kernel, layer=i, cp=cp, n_pages=n_pages,
                                    scale=mla_qk ** -0.5, rope=mla_rope),
                  page_table, qm, [c_new, kpe_new], [], [cache_mla_ckv, kpe_t], [], g_out=hm)
        of_s = da(functools.partial(_dec_fox_kernel, layer=i, cp=cp, n_pages=n_pages, scale=fox_dim ** -0.5),
                  page_table, qf, [kf_new, vf_new, logf_new], [], [cache_fox_k, cache_fox_v, logf_t],
                  [pltpu.VMEM((hf, 1), _F32), pltpu.VMEM((n_new, _LANE), _F32)], g_out=hf)
        lam_spec2 = pl.BlockSpec((8, _LANE), lambda b, c, pt: (0, 0))
        od_s = da(functools.partial(_dec_diff_kernel, layer=i, cp=cp, n_pages=n_pages,
                                    scale=dqk ** -0.5, lam_init=lam_init),
                  page_table, qd, [kd_new, vd_new], [(lamv, lam_spec2)], [dk_t, cache_diff_v], [], g_out=hd)

        om = jnp.concatenate([om_p, om_s], axis=1)
        of = jnp.concatenate([of_p, of_s], axis=1)
        od = jnp.concatenate([od_p, od_s], axis=1)
        h = _merge(h, om, of, od, wuv, g_diff_sub[i][None], w_out[i].astype(_BF16), tm=tm,
                   sub_scale=1.0 - lam_init)
        h = _ffn(h, g_ffn[i][None], w_gate[i].astype(_BF16), w_up[i].astype(_BF16),
                 w_down[i].astype(_BF16), tm=tm_ffn, tf=tf)
        p = jnp.concatenate([p_prompt[i].reshape(t_p, -1), p_sample[i].reshape(t_s, -1)], axis=0)
        h = _ple(h, p, g_ple[i][None], w_ple_gate[i].astype(_BF16), w_ple_proj[i].astype(_BF16), tm=tm)

        for j, r in enumerate((c_new, kpe_new[:, :mla_rope], kf_new, vf_new, logf_new[:, :hf], kd_new, vd_new)):
            rows[j].append(r)

    outs = [h[:t_p].reshape(batch, seq, d), h[t_p:].reshape(dec_batch, n_new, d)]
    for j, rs in enumerate(rows):
        st = jnp.stack(rs)
        tail = (2, dqk) if j == 5 else (st.shape[-1],)
        outs.append(st[:, :t_p].reshape((depth, batch, seq) + tail))
        outs.append(st[:, t_p:].reshape((depth, dec_batch, n_new) + tail))
    return tuple(outs)
```

```python
import functools
import math

import jax
import jax.numpy as jnp
from jax import lax
from jax.experimental import pallas as pl
from jax.experimental.pallas import tpu as pltpu

_F32 = jnp.float32
_BF16 = jnp.bfloat16
_LANE = 128
_VMEM_LIMIT = 52 * 1024 * 1024
_NORM_EPS = 1e-6
_ROPE_THETA = 10000.0
_NEG_INF = float("-inf")
_PROMPT_Q_TILE = 128
_PROMPT_K_TILE = 512
_PAGES_PER_STEP = 128


def _cparams(n_grid):
    return pltpu.CompilerParams(dimension_semantics=("arbitrary",) * n_grid,
                                vmem_limit_bytes=_VMEM_LIMIT)


def _rms(x, n):
    ss = jnp.sum(x * x, axis=-1, keepdims=True)
    return x * lax.rsqrt(ss * (1.0 / n) + _NORM_EPS)


def _rope(x, cos, sin_a, sin_b, half):
    lanes = x.shape[-1]
    return x * cos + pltpu.roll(x, half, 1) * sin_a + pltpu.roll(x, lanes - half, 1) * sin_b


def _split3(x):
    hi = x.astype(_BF16)
    r1 = x - hi.astype(_F32)
    mid = r1.astype(_BF16)
    lo = (r1 - mid.astype(_F32)).astype(_BF16)
    return hi, mid, lo


def _tri(n, kind):
    r = lax.broadcasted_iota(jnp.int32, (n, n), 0)
    c = lax.broadcasted_iota(jnp.int32, (n, n), 1)
    if kind == "lower_incl":
        m = c <= r
    else:
        m = r > c
    return jnp.where(m, 1.0, 0.0).astype(_BF16)


def _online_update(s, v_bf, m_sc, l_sc, acc_sc):
    m_prev = m_sc[...]
    m_new = jnp.maximum(m_prev, jnp.max(s, axis=-1, keepdims=True))
    alpha = jnp.exp(m_prev - m_new)
    p = jnp.exp(s - m_new)
    l_sc[...] = alpha * l_sc[...] + jnp.sum(p, axis=-1, keepdims=True)
    acc_sc[...] = alpha * acc_sc[...] + jnp.dot(p.astype(_BF16), v_bf, preferred_element_type=_F32)
    m_sc[...] = m_new


def _nt_dot(a, b):
    return lax.dot_general(a, b, (((1,), (1,)), ((), ())), preferred_element_type=_F32)


def _inproj_kernel(x_ref, gattn_ref, wm_ref, wf_ref, wd_ref, wuk_ref, gv_ref, tab_ref,
                   qm_ref, qf_ref, qd_ref, c_ref, kpe_ref, kf_ref, vf_ref, logf_ref, kd_ref, vd_ref,
                   *, hm, hf, hd, mla_qk, mla_rope, dqk):
    x = x_ref[...]
    u = (_rms(x, x.shape[-1]) * gattn_ref[...]).astype(_BF16)
    g_nope, g_rope, g_ckv, g_kpe = gv_ref[0:1], gv_ref[1:2], gv_ref[2:3], gv_ref[3:4]
    b_fox, g_fq, g_fk, g_dq, g_dk = gv_ref[4:5], gv_ref[5:6], gv_ref[6:7], gv_ref[7:8], gv_ref[8:9]
    cos_m, sin_ma, sin_mb = tab_ref[0], tab_ref[1], tab_ref[2]
    cos_d, sin_da, sin_db = tab_ref[3], tab_ref[4], tab_ref[5]
    half_m, half_d = mla_rope // 2, dqk // 2

    pm = _nt_dot(u, wm_ref[...])
    for h in range(hm):
        nope = pm[:, 2 * _LANE * h:2 * _LANE * h + _LANE]
        rp = pm[:, 2 * _LANE * h + _LANE:2 * _LANE * (h + 1)]
        ss = jnp.sum(nope * nope, axis=-1, keepdims=True) + jnp.sum(rp * rp, axis=-1, keepdims=True)
        r = lax.rsqrt(ss * (1.0 / mla_qk) + _NORM_EPS)
        nn = nope * r * g_nope
        rr = _rope(rp * r * g_rope, cos_m, sin_ma, sin_mb, half_m)
        qm_ref[h, :, 0:_LANE] = jnp.dot(nn.astype(_BF16), wuk_ref[h], preferred_element_type=_F32)
        qm_ref[h, :, _LANE:2 * _LANE] = rr
    base = 2 * _LANE * hm
    c_ref[...] = _rms(pm[:, base:base + _LANE], _LANE) * g_ckv
    kr = _rms(pm[:, base + _LANE:base + 2 * _LANE], mla_rope) * g_kpe
    kpe_ref[...] = _rope(kr, cos_m, sin_ma, sin_mb, half_m)

    pf = _nt_dot(u, wf_ref[...])
    z = pf[:, 0:_LANE] + b_fox
    logf_ref[...] = jnp.minimum(z, 0.0) - jnp.log1p(jnp.exp(-jnp.abs(z)))
    for h in range(hf):
        qf_ref[h] = _rms(pf[:, _LANE * (1 + h):_LANE * (2 + h)], _LANE) * g_fq
    base = _LANE * (1 + hf)
    kf_ref[...] = _rms(pf[:, base:base + _LANE], _LANE) * g_fk
    vf_ref[...] = pf[:, base + _LANE:base + 2 * _LANE]

    pd = _nt_dot(u, wd_ref[...])
    for j in range(2 * hd):
        qn = _rms(pd[:, _LANE * j:_LANE * (j + 1)], dqk) * g_dq
        qd_ref[j] = _rope(qn, cos_d, sin_da, sin_db, half_d)
    base = _LANE * 2 * hd
    k = pd[:, base:base + _LANE]
    first = lax.broadcasted_iota(jnp.int32, k.shape, 1) < dqk
    k2 = k * k
    ss0 = jnp.sum(jnp.where(first, k2, 0.0), axis=-1, keepdims=True)
    ss1 = jnp.sum(jnp.where(first, 0.0, k2), axis=-1, keepdims=True)
    r = lax.rsqrt(jnp.where(first, ss0, ss1) * (1.0 / dqk) + _NORM_EPS)
    kd_ref[...] = _rope(k * r * g_dk, cos_d, sin_da, sin_db, half_d)
    vd_ref[...] = pd[:, base + _LANE:base + 2 * _LANE]


def _inproj(x, gattn, wm, wf, wd, wuk, gv, tabs, *, tm, hm, hf, hd, mla_qk, mla_rope, dqk):
    t, d = x.shape
    const2 = lambda i: (0, 0)
    const3 = lambda i: (0, 0, 0)
    row = lambda i: (i, 0)
    hrow = lambda i: (0, i, 0)
    w_spec = lambda w: pl.BlockSpec(w.shape, const2 if w.ndim == 2 else const3,
                                    pipeline_mode=pl.Buffered(1))
    tok = pl.BlockSpec((tm, _LANE), row)
    tok_shape = jax.ShapeDtypeStruct((t, _LANE), _F32)
    kern = functools.partial(_inproj_kernel, hm=hm, hf=hf, hd=hd, mla_qk=mla_qk,
                             mla_rope=mla_rope, dqk=dqk)
    return pl.pallas_call(
        kern,
        grid=(t // tm,),
        in_specs=[pl.BlockSpec((tm, d), row), w_spec(gattn), w_spec(wm), w_spec(wf), w_spec(wd),
                  w_spec(wuk), w_spec(gv), pl.BlockSpec((tabs.shape[0], tm, _LANE), hrow)],
        out_specs=[pl.BlockSpec((hm, tm, 2 * _LANE), hrow), pl.BlockSpec((hf, tm, _LANE), hrow),
                   pl.BlockSpec((2 * hd, tm, _LANE), hrow)] + [tok] * 7,
        out_shape=[jax.ShapeDtypeStruct((hm, t, 2 * _LANE), _F32),
                   jax.ShapeDtypeStruct((hf, t, _LANE), _F32),
                   jax.ShapeDtypeStruct((2 * hd, t, _LANE), _F32)] + [tok_shape] * 7,
        compiler_params=_cparams(1),
        name="inproj",
    )(x, gattn, wm, wf, wd, wuk, gv, tabs)


def _cumsum_kernel(x_ref, col_ref, row_ref, carry_sc):
    @pl.when(pl.program_id(1) == 0)
    def _():
        carry_sc[...] = jnp.zeros_like(carry_sc)

    n = x_ref.shape[0]
    tri = _tri(n, "lower_incl")
    cum = carry_sc[...] + sum(jnp.dot(tri, p, preferred_element_type=_F32) for p in _split3(x_ref[...]))
    col_ref[...] = cum
    carry_sc[...] = cum[n - 1:n, :]
    row_ref[0] = cum.T[0:row_ref.shape[1], :]


def _prompt_cumsum(logf, *, batch, seq, blk):
    nb = seq // blk
    return pl.pallas_call(
        _cumsum_kernel,
        grid=(batch, nb),
        in_specs=[pl.BlockSpec((blk, _LANE), lambda b, i: (b * nb + i, 0))],
        out_specs=[pl.BlockSpec((blk, _LANE), lambda b, i: (b * nb + i, 0)),
                   pl.BlockSpec((1, 8, blk), lambda b, i: (b, 0, i))],
        out_shape=[jax.ShapeDtypeStruct((batch * seq, _LANE), _F32),
                   jax.ShapeDtypeStruct((batch, 8, seq), _F32)],
        scratch_shapes=[pltpu.VMEM((1, _LANE), _F32)],
        compiler_params=_cparams(2),
        name="prompt_cumsum",
    )(logf)


def _causal_mask(s, g, tq, tk, qi, ki):
    s3 = s.reshape(g, tq, tk)
    row = qi * tq + lax.broadcasted_iota(jnp.int32, (1, tq, tk), 1)
    col = ki * tk + lax.broadcasted_iota(jnp.int32, (1, tq, tk), 2)
    return jnp.where(col <= row, s3, _NEG_INF).reshape(g * tq, tk)


def _pa_flash(qt_ref, kt_ref, q_ref, tk, m_sc, l_sc, acc_sc, scores, values, finalize):
    step_id = pl.program_id(1)
    qi, ki = qt_ref[step_id], kt_ref[step_id]
    g, tq, dq = q_ref.shape
    last = (qi * tq + tq - 1) // tk

    @pl.when(ki == 0)
    def _():
        _flash_init(m_sc, l_sc, acc_sc)

    def step(diag):
        s = scores(q_ref[...].reshape(g * tq, dq).astype(_BF16))
        if diag:
            s = _causal_mask(s, g, tq, tk, qi, ki)
        _online_update(s, values(), m_sc, l_sc, acc_sc)

    @pl.when(ki < last)
    def _():
        step(False)

    @pl.when(ki == last)
    def _():
        step(True)
        finalize()


def _flash_init(m_sc, l_sc, acc_sc):
    m_sc[...] = jnp.full_like(m_sc, _NEG_INF)
    l_sc[...] = jnp.zeros_like(l_sc)
    acc_sc[...] = jnp.zeros_like(acc_sc)


def _pa_mla_kernel(qt_ref, kt_ref, q_ref, c_ref, kpe_ref, o_ref, m_sc, l_sc, acc_sc, *, scale):
    def scores(q):
        k = jnp.concatenate([c_ref[...], kpe_ref[...]], axis=-1).astype(_BF16)
        return _nt_dot(q, k) * scale

    def finalize():
        o_ref[...] = (acc_sc[...] / l_sc[...]).reshape(o_ref.shape)

    _pa_flash(qt_ref, kt_ref, q_ref, c_ref.shape[0], m_sc, l_sc, acc_sc, scores,
              lambda: c_ref[...].astype(_BF16), finalize)


def _pa_fox_kernel(qt_ref, kt_ref, q_ref, k_ref, v_ref, fq_ref, fk_ref, o_ref, m_sc, l_sc, acc_sc,
                   *, scale):
    g, tq, _ = q_ref.shape

    def scores(q):
        s = _nt_dot(q, k_ref[...].astype(_BF16)) * scale
        fq = fq_ref[...]
        fk = fk_ref[0]
        return jnp.concatenate(
            [s[h * tq:(h + 1) * tq] + (fq[:, h:h + 1] - fk[h:h + 1, :]) for h in range(g)], axis=0)

    def finalize():
        o_ref[...] = (acc_sc[...] / l_sc[...]).reshape(o_ref.shape)

    _pa_flash(qt_ref, kt_ref, q_ref, k_ref.shape[0], m_sc, l_sc, acc_sc, scores,
              lambda: v_ref[...].astype(_BF16), finalize)


def _diff_lambda(lam_ref, lam_init):
    a = jnp.sum(lam_ref[0:1] * lam_ref[1:2], axis=-1, keepdims=True)
    b = jnp.sum(lam_ref[2:3] * lam_ref[3:4], axis=-1, keepdims=True)
    return jnp.exp(a) - jnp.exp(b) + lam_init


def _diff_combine(acc, l, lam, half):
    w = acc / l
    return w[0:half] - lam * w[half:2 * half]


def _pa_diff_kernel(qt_ref, kt_ref, q_ref, k_ref, v_ref, lam_ref, o_ref, m_sc, l_sc, acc_sc,
                    *, scale, lam_init):
    g, tq, _ = q_ref.shape

    def finalize():
        lam = _diff_lambda(lam_ref, lam_init)
        o_ref[...] = _diff_combine(acc_sc[...], l_sc[...], lam, (g // 2) * tq).reshape(o_ref.shape)

    _pa_flash(qt_ref, kt_ref, q_ref, k_ref.shape[0], m_sc, l_sc, acc_sc,
              lambda q: _nt_dot(q, k_ref[...].astype(_BF16)) * scale,
              lambda: v_ref[...].astype(_BF16), finalize)


def _prompt_attn(kern, q, kv_list, extra, *, batch, seq, tq, tk, g_out):
    g, _, dq = q.shape
    nq, nk = seq // tq, seq // tk
    pairs = [(qi, ki) for qi in range(nq) for ki in range((qi * tq + tq - 1) // tk + 1)]
    qt = jnp.asarray([p[0] for p in pairs], jnp.int32)
    kt = jnp.asarray([p[1] for p in pairs], jnp.int32)

    def spec(block, index):
        return pl.BlockSpec(block, lambda b, s, qt_ref, kt_ref: index(b, qt_ref[s], kt_ref[s]))

    grid_spec = pltpu.PrefetchScalarGridSpec(
        num_scalar_prefetch=2,
        grid=(batch, len(pairs)),
        in_specs=[spec((g, tq, dq), lambda b, qi, ki: (0, b * nq + qi, 0))]
                 + [spec((tk, _LANE), lambda b, qi, ki: (b * nk + ki, 0))] * len(kv_list)
                 + [spec(block, index) for _, block, index in extra],
        out_specs=spec((g_out, tq, _LANE), lambda b, qi, ki: (0, b * nq + qi, 0)),
        scratch_shapes=[pltpu.VMEM((g * tq, 1), _F32), pltpu.VMEM((g * tq, 1), _F32),
                        pltpu.VMEM((g * tq, _LANE), _F32)],
    )
    return pl.pallas_call(
        kern,
        grid_spec=grid_spec,
        out_shape=jax.ShapeDtypeStruct((g_out, batch * seq, _LANE), _F32),
        compiler_params=_cparams(2),
        name=kern.func.__name__.strip("_"),
    )(qt, kt, q, *kv_list, *[a for a, _, _ in extra])


def _page_copies(pt_ref, chunk, slot, layer, cp, reverse, n_pages, hbm_refs, bufs, sems):
    copies = []
    chunks_per_seq = n_pages // cp
    for i in range(cp):
        if reverse:
            b = chunk // chunks_per_seq
            c = chunk % chunks_per_seq
            idx = b * n_pages + (n_pages - 1 - (c * cp + i))
        else:
            idx = chunk * cp + i
        page = pt_ref[idx]
        for a, (hbm, buf) in enumerate(zip(hbm_refs, bufs)):
            copies.append(pltpu.make_async_copy(hbm.at[layer, page], buf.at[slot, i], sems.at[slot, a]))
    return copies


def _decode_pipeline(pt_ref, layer, cp, reverse, n_pages, hbm_refs, bufs, sems):
    nc = pl.num_programs(1)
    g = pl.program_id(0) * nc + pl.program_id(1)
    total = pl.num_programs(0) * nc
    slot = g % 2
    args = (layer, cp, reverse, n_pages, hbm_refs, bufs, sems)

    @pl.when(g == 0)
    def _():
        for cpy in _page_copies(pt_ref, g, slot, *args):
            cpy.start()

    @pl.when(g + 1 < total)
    def _():
        for cpy in _page_copies(pt_ref, g + 1, 1 - slot, *args):
            cpy.start()

    for cpy in _page_copies(pt_ref, g, slot, *args):
        cpy.wait()
    return slot


def _new_token_mask(s, n_new):
    t = lax.broadcasted_iota(jnp.int32, s.shape, 0) % n_new
    col = lax.broadcasted_iota(jnp.int32, s.shape, 1)
    return jnp.where(col <= t, s, _NEG_INF)


def _pad_rows(x, rows):
    return jnp.concatenate([x, jnp.zeros((rows - x.shape[0], x.shape[1]), x.dtype)], axis=0)


def _dec_mla_kernel(pt_ref, q_ref, cn_ref, kn_ref, ckv_hbm, kpe_hbm, o_ref,
                    ckv_buf, kpe_buf, sems, m_sc, l_sc, acc_sc, *, layer, cp, n_pages, scale, rope):
    c_idx = pl.program_id(1)
    g, n_new, dq = q_ref.shape
    slot = _decode_pipeline(pt_ref, layer, cp, False, n_pages, (ckv_hbm, kpe_hbm),
                            (ckv_buf, kpe_buf), sems)
    q = q_ref[...].reshape(g * n_new, dq)
    q_lat = q[:, 0:_LANE].astype(_BF16)
    q_pe = q[:, _LANE:_LANE + rope].astype(_BF16)

    @pl.when(c_idx == 0)
    def _():
        _flash_init(m_sc, l_sc, acc_sc)
        cn = _pad_rows(cn_ref[...], _LANE)
        kn = jnp.concatenate([cn, _pad_rows(kn_ref[...], _LANE)], axis=-1).astype(_BF16)
        s = _new_token_mask(_nt_dot(q.astype(_BF16), kn) * scale, n_new)
        _online_update(s, cn.astype(_BF16), m_sc, l_sc, acc_sc)

    c = ckv_buf[slot].reshape(cp * _LANE, _LANE).astype(_BF16)
    s_pe = jnp.concatenate(
        [jnp.dot(q_pe, kpe_buf[slot, i].astype(_BF16), preferred_element_type=_F32) for i in range(cp)],
        axis=-1)
    s = (_nt_dot(q_lat, c) + s_pe) * scale
    _online_update(s, c, m_sc, l_sc, acc_sc)

    @pl.when(c_idx == pl.num_programs(1) - 1)
    def _():
        o_ref[...] = (acc_sc[...] / l_sc[...]).reshape(o_ref.shape)


def _dec_fox_kernel(pt_ref, q_ref, kn_ref, vn_ref, fn_ref, k_hbm, v_hbm, f_hbm, o_ref,
                    k_buf, v_buf, f_buf, sems, m_sc, l_sc, acc_sc, carry_sc, ncum_sc,
                    *, layer, cp, n_pages, scale):
    c_idx = pl.program_id(1)
    g, n_new, dq = q_ref.shape
    slot = _decode_pipeline(pt_ref, layer, cp, True, n_pages, (k_hbm, v_hbm, f_hbm),
                            (k_buf, v_buf, f_buf), sems)
    q = q_ref[...].reshape(g * n_new, dq).astype(_BF16)

    def head_rows(per_head):
        return jnp.concatenate([jnp.broadcast_to(per_head(h), (n_new, per_head(h).shape[-1]))
                                for h in range(g)], axis=0)

    @pl.when(c_idx == 0)
    def _():
        _flash_init(m_sc, l_sc, acc_sc)
        carry_sc[...] = jnp.zeros_like(carry_sc)
        fn = _pad_rows(fn_ref[...], _LANE)
        ncol = sum(jnp.dot(_tri(_LANE, "lower_incl"), p, preferred_element_type=_F32)
                   for p in _split3(fn))
        ncum_sc[...] = ncol[0:n_new, :]
        nrow = ncol.T
        kn = _pad_rows(kn_ref[...], _LANE).astype(_BF16)
        s = _nt_dot(q, kn) * scale
        s = s + head_rows(lambda h: ncol[0:n_new, h:h + 1] - nrow[h:h + 1, :])
        s = _new_token_mask(s, n_new)
        _online_update(s, _pad_rows(vn_ref[...], _LANE).astype(_BF16), m_sc, l_sc, acc_sc)

    k = k_buf[slot].reshape(cp * _LANE, _LANE).astype(_BF16)
    s = _nt_dot(q, k) * scale
    sgt = _tri(_LANE, "row_gt_col")
    run = carry_sc[...]
    biases = []
    for i in range(cp):
        f = f_buf[slot, i]
        within = sum(jnp.dot(p, sgt, preferred_element_type=_F32) for p in _split3(f))
        biases.append(run + within)
        run = run + jnp.sum(f, axis=-1, keepdims=True)
    carry_sc[...] = run
    bias = jnp.concatenate(biases, axis=-1)
    ncum = ncum_sc[...]
    s = s + head_rows(lambda h: bias[h:h + 1, :]) + jnp.concatenate(
        [ncum[:, h:h + 1] for h in range(g)], axis=0)
    _online_update(s, v_buf[slot].reshape(cp * _LANE, _LANE).astype(_BF16), m_sc, l_sc, acc_sc)

    @pl.when(c_idx == pl.num_programs(1) - 1)
    def _():
        o_ref[...] = (acc_sc[...] / l_sc[...]).reshape(o_ref.shape)


def _dec_diff_kernel(pt_ref, q_ref, kn_ref, vn_ref, lam_ref, k_hbm, v_hbm, o_ref,
                     k_buf, v_buf, sems, m_sc, l_sc, acc_sc,
                     *, layer, cp, n_pages, scale, lam_init):
    c_idx = pl.program_id(1)
    g, n_new, dq = q_ref.shape
    slot = _decode_pipeline(pt_ref, layer, cp, False, n_pages, (k_hbm, v_hbm), (k_buf, v_buf), sems)
    q = q_ref[...].reshape(g * n_new, dq).astype(_BF16)

    @pl.when(c_idx == 0)
    def _():
        _flash_init(m_sc, l_sc, acc_sc)
        kn = _pad_rows(kn_ref[...], _LANE).astype(_BF16)
        s = _new_token_mask(_nt_dot(q, kn) * scale, n_new)
        _online_update(s, _pad_rows(vn_ref[...], _LANE).astype(_BF16), m_sc, l_sc, acc_sc)

    s = jnp.concatenate(
        [jnp.dot(q, k_buf[slot, i].astype(_BF16), preferred_element_type=_F32) for i in range(cp)],
        axis=-1) * scale
    _online_update(s, v_buf[slot].reshape(cp * _LANE, _LANE).astype(_BF16), m_sc, l_sc, acc_sc)

    @pl.when(c_idx == pl.num_programs(1) - 1)
    def _():
        lam = _diff_lambda(lam_ref, lam_init)
        o_ref[...] = _diff_combine(acc_sc[...], l_sc[...], lam, (g // 2) * n_new).reshape(o_ref.shape)


def _decode_attn(kern, page_table, q, new_rows, extra, caches, extra_scratch, *,
                 tok0, n_new, cp, g_out):
    g, _, dq = q.shape
    dec_batch, n_pages = page_table.shape
    blk0 = tok0 // n_new
    new_spec = pl.BlockSpec((n_new, _LANE), lambda b, c, pt: (blk0 + b, 0))
    grid_spec = pltpu.PrefetchScalarGridSpec(
        num_scalar_prefetch=1,
        grid=(dec_batch, n_pages // cp),
        in_specs=[pl.BlockSpec((g, n_new, dq), lambda b, c, pt: (0, blk0 + b, 0))]
                 + [new_spec] * len(new_rows) + [s for _, s in extra]
                 + [pl.BlockSpec(memory_space=pl.ANY)] * len(caches),
        out_specs=pl.BlockSpec((g_out, n_new, _LANE), lambda b, c, pt: (0, b, 0)),
        scratch_shapes=[pltpu.VMEM((2, cp) + c.shape[2:], _F32) for c in caches]
                       + [pltpu.SemaphoreType.DMA((2, len(caches))),
                          pltpu.VMEM((g * n_new, 1), _F32), pltpu.VMEM((g * n_new, 1), _F32),
                          pltpu.VMEM((g * n_new, _LANE), _F32)] + list(extra_scratch),
    )
    return pl.pallas_call(
        kern,
        grid_spec=grid_spec,
        out_shape=jax.ShapeDtypeStruct((g_out, dec_batch * n_new, _LANE), _F32),
        compiler_params=_cparams(2),
        name=kern.func.__name__.strip("_"),
    )(page_table.reshape(-1), q, *new_rows, *[a for a, _ in extra], *caches)


def _merge_kernel(h_ref, om_ref, of_ref, od_ref, wuv_ref, gsub_ref, wout_ref, o_ref, *, sub_scale):
    parts = []
    for h in range(om_ref.shape[0]):
        parts.append(jnp.dot(om_ref[h].astype(_BF16), wuv_ref[h],
                             preferred_element_type=_F32).astype(_BF16))
    for h in range(of_ref.shape[0]):
        parts.append(of_ref[h].astype(_BF16))
    for h in range(od_ref.shape[0]):
        parts.append((_rms(od_ref[h], _LANE) * gsub_ref[...] * sub_scale).astype(_BF16))
    mix = jnp.concatenate(parts, axis=-1)
    o_ref[...] = h_ref[...] + jnp.dot(mix, wout_ref[...], preferred_element_type=_F32)


def _merge(h, om, of, od, wuv, gsub, wout, *, tm, sub_scale):
    t, d = h.shape
    row = lambda i: (i, 0)
    hrow = lambda i: (0, i, 0)
    const = lambda a: pl.BlockSpec(a.shape, (lambda i: (0,) * a.ndim), pipeline_mode=pl.Buffered(1))
    return pl.pallas_call(
        functools.partial(_merge_kernel, sub_scale=sub_scale),
        grid=(t // tm,),
        in_specs=[pl.BlockSpec((tm, d), row)]
                 + [pl.BlockSpec((a.shape[0], tm, _LANE), hrow) for a in (om, of, od)]
                 + [const(wuv), const(gsub), const(wout)],
        out_specs=pl.BlockSpec((tm, d), row),
        out_shape=jax.ShapeDtypeStruct((t, d), _F32),
        compiler_params=_cparams(1),
        name="merge",
    )(h, om, of, od, wuv, gsub, wout)


def _ffn_kernel(x_ref, g_ref, wg_ref, wu_ref, wd_ref, o_ref, u_sc):
    @pl.when(pl.program_id(1) == 0)
    def _():
        x = x_ref[...]
        u_sc[...] = (_rms(x, x.shape[-1]) * g_ref[...]).astype(_BF16)
        o_ref[...] = x

    u = u_sc[...]
    gate = jnp.dot(u, wg_ref[...], preferred_element_type=_F32)
    up = jnp.dot(u, wu_ref[...], preferred_element_type=_F32)
    act = (gate * jax.nn.sigmoid(gate) * up).astype(_BF16)
    o_ref[...] += jnp.dot(act, wd_ref[...], preferred_element_type=_F32)


def _ffn(h, g, wg, wu, wd, *, tm, tf):
    t, d = h.shape
    f = wg.shape[1]
    return pl.pallas_call(
        _ffn_kernel,
        grid=(t // tm, f // tf),
        in_specs=[pl.BlockSpec((tm, d), lambda i, j: (i, 0)),
                  pl.BlockSpec((1, d), lambda i, j: (0, 0)),
                  pl.BlockSpec((d, tf), lambda i, j: (0, j)),
                  pl.BlockSpec((d, tf), lambda i, j: (0, j)),
                  pl.BlockSpec((tf, d), lambda i, j: (j, 0))],
        out_specs=pl.BlockSpec((tm, d), lambda i, j: (i, 0)),
        out_shape=jax.ShapeDtypeStruct((t, d), _F32),
        scratch_shapes=[pltpu.VMEM((tm, d), _BF16)],
        compiler_params=_cparams(2),
        name="ffn",
    )(h, g, wg, wu, wd)


def _ple_kernel(h_ref, p_ref, g_ref, wg_ref, wp_ref, o_ref):
    h = h_ref[...]
    u = (_rms(h, h.shape[-1]) * g_ref[...]).astype(_BF16)
    gate = jax.nn.sigmoid(jnp.dot(u, wg_ref[...], preferred_element_type=_F32))
    o_ref[...] = h + gate * jnp.dot(p_ref[...].astype(_BF16), wp_ref[...], preferred_element_type=_F32)


def _ple(h, p, g, wg, wp, *, tm):
    t, d = h.shape
    row = lambda i: (i, 0)
    const = lambda a: pl.BlockSpec(a.shape, (lambda i: (0, 0)), pipeline_mode=pl.Buffered(1))
    return pl.pallas_call(
        _ple_kernel,
        grid=(t // tm,),
        in_specs=[pl.BlockSpec((tm, d), row), pl.BlockSpec((tm, p.shape[1]), row),
                  const(g), const(wg), const(wp)],
        out_specs=pl.BlockSpec((tm, d), row),
        out_shape=jax.ShapeDtypeStruct((t, d), _F32),
        compiler_params=_cparams(1),
        name="ple",
    )(h, p, g, wg, wp)


def _pad_lanes(a, left=0, width=_LANE):
    pad = [(0, 0)] * (a.ndim - 1) + [(left, width - left - a.shape[-1])]
    return jnp.pad(a, pad)


def _pad_axis(a, axis, before=0, total=_LANE):
    pad = [(0, 0)] * a.ndim
    pad[axis] = (before, total - before - a.shape[axis])
    return jnp.pad(a, pad)


def _rope_tables(pos, half, reps):
    inv_freq = jnp.exp(-math.log(_ROPE_THETA) * jnp.arange(half, dtype=_F32) / half)
    ang = pos.astype(_F32)[:, None] * inv_freq[None, :]
    cos, sin, zero = jnp.cos(ang), jnp.sin(ang), jnp.zeros_like(ang)
    tile = lambda a, b: _pad_lanes(jnp.concatenate([a, b] * reps, axis=-1))
    return tile(cos, cos), tile(zero, sin), tile(-sin, zero)


def _tile_of(n, want):
    t = min(n, want)
    while n % t:
        t //= 2
    return t


def kernel(x_prompt, x_sample, cache_mla_ckv, cache_mla_kpe, cache_fox_k, cache_fox_v, cache_fox_logf, cache_diff_k, cache_diff_v, page_table, p_prompt, p_sample, g_attn, w_in, g_mla_q, g_ckv, g_kpe, w_uk, w_uv, g_fox_q, g_fox_k, b_fox, g_diff_q, g_diff_k, lam_q1, lam_k1, lam_q2, lam_k2, g_diff_sub, w_out, g_ffn, w_gate, w_up, w_down, g_ple, w_ple_gate, w_ple_proj):
    batch, seq, d = x_prompt.shape
    dec_batch, n_new, _ = x_sample.shape
    depth = w_in.shape[0]
    page = cache_mla_ckv.shape[2]
    kv_rank = cache_mla_ckv.shape[3]
    mla_rope = cache_mla_kpe.shape[3]
    hm, nope = w_uk.shape[2], w_uk.shape[3]
    mla_v = w_uv.shape[3]
    mla_qk = nope + mla_rope
    fox_dim = cache_fox_k.shape[3]
    hf = cache_fox_logf.shape[3]
    dqk = cache_diff_k.shape[4]
    dv = cache_diff_v.shape[3]
    hd = (w_in.shape[2] - (hm * mla_qk + kv_rank + mla_rope + hf * fox_dim + 2 * fox_dim + hf
                           + 2 * dqk + dv)) // (2 * dqk)
    n_pages = page_table.shape[1]
    past_len = n_pages * page
    assert page == kv_rank == nope == mla_v == fox_dim == dv == 2 * dqk == _LANE
    assert n_new == 8 and mla_rope % 2 == 0 and mla_rope <= _LANE and hf <= 8

    t_p, t_s = batch * seq, dec_batch * n_new
    t = t_p + t_s
    tm = _tile_of(math.gcd(t_p, t_s), 256)
    tq = _tile_of(seq, _PROMPT_Q_TILE)
    tk = _tile_of(seq, _PROMPT_K_TILE)
    cp = _tile_of(n_pages, _PAGES_PER_STEP)
    tf = _tile_of(w_gate.shape[2], 512)
    tm_ffn = _tile_of(math.gcd(t_p, t_s), 512)

    pos = jnp.concatenate([jnp.tile(jnp.arange(seq, dtype=jnp.int32), batch),
                           jnp.tile(past_len + jnp.arange(n_new, dtype=jnp.int32), dec_batch)])
    tabs = jnp.stack(_rope_tables(pos, mla_rope // 2, 1) + _rope_tables(pos, dqk // 2, _LANE // dqk))

    kpe_t = jnp.transpose(cache_mla_kpe, (0, 1, 3, 2))
    logf_t = jnp.transpose(cache_fox_logf, (0, 1, 3, 2))
    dk_t = jnp.transpose(cache_diff_k, (0, 1, 3, 4, 2)).reshape(cache_diff_k.shape[:2] + (2 * dqk, page))

    w_in_t = jnp.transpose(w_in, (2, 0, 1))
    h = jnp.concatenate([x_prompt.reshape(t_p, d), x_sample.reshape(t_s, d)], axis=0)
    offs = [0]
    for n in (hm * mla_qk, kv_rank, mla_rope, hf * fox_dim, fox_dim, fox_dim, hf,
              hd * 2 * dqk, 2 * dqk, dv):
        offs.append(offs[-1] + n)
    rows = [[] for _ in range(7)]
    for i in range(depth):
        w = w_in_t[:, i, :]
        blk = lambda j: w[offs[j]:offs[j + 1]]
        wq = blk(0).reshape(hm, mla_qk, d)
        wq = jnp.concatenate([wq[:, :nope], _pad_axis(wq[:, nope:], 1)], axis=1).reshape(hm * 2 * _LANE, d)
        wm = jnp.concatenate([wq, blk(1), _pad_axis(blk(2), 0)], axis=0).astype(_BF16)
        wf = jnp.concatenate([_pad_axis(blk(6), 0), blk(3), blk(4), blk(5)], axis=0).astype(_BF16)
        wdq = blk(7).reshape(hd, 2, dqk, d)
        wdq = jnp.concatenate([_pad_axis(wdq[:, 0], 1), _pad_axis(wdq[:, 1], 1, before=dqk)], axis=0)
        wd = jnp.concatenate([wdq.reshape(2 * hd * _LANE, d), blk(8), blk(9)], axis=0).astype(_BF16)
        wuk = jnp.transpose(w_uk[i], (1, 2, 0)).astype(_BF16)
        wuv = jnp.transpose(w_uv[i], (1, 0, 2)).astype(_BF16)
        gv = jnp.stack([g_mla_q[i][:nope], _pad_lanes(g_mla_q[i][nope:]), g_ckv[i], _pad_lanes(g_kpe[i]),
                        _pad_lanes(b_fox[i]), g_fox_q[i], g_fox_k[i],
                        jnp.tile(g_diff_q[i], 2), jnp.tile(g_diff_k[i], 2)]
                       + [jnp.zeros((_LANE,), _F32)] * 7)
        lamv = jnp.stack([_pad_lanes(v) for v in (lam_q1[i], lam_k1[i], lam_q2[i], lam_k2[i])]
                         + [jnp.zeros((_LANE,), _F32)] * 4)
        lam_init = 0.8 - 0.6 * math.exp(-0.3 * i)

        qm, qf, qd, c_new, kpe_new, kf_new, vf_new, logf_new, kd_new, vd_new = _inproj(
            h, g_attn[i][None], wm, wf, wd, wuk, gv, tabs, tm=tm, hm=hm, hf=hf, hd=hd,
            mla_qk=mla_qk, mla_rope=mla_rope, dqk=dqk)

        nq = seq // tq
        fc_col, fc_row = _prompt_cumsum(logf_new, batch=batch, seq=seq, blk=_LANE)
        pa = functools.partial(_prompt_attn, batch=batch, seq=seq, tq=tq, tk=tk)
        om_p = pa(functools.partial(_pa_mla_kernel, scale=mla_qk ** -0.5), qm, [c_new, kpe_new], [], g_out=hm)
        of_p = pa(functools.partial(_pa_fox_kernel, scale=fox_dim ** -0.5), qf, [kf_new, vf_new],
                  [(fc_col, (tq, _LANE), lambda b, qi, ki: (b * nq + qi, 0)),
                   (fc_row, (1, 8, tk), lambda b, qi, ki: (b, 0, ki))],
                  g_out=hf)
        od_p = pa(functools.partial(_pa_diff_kernel, scale=dqk ** -0.5, lam_init=lam_init), qd,
                  [kd_new, vd_new], [(lamv, (8, _LANE), lambda b, qi, ki: (0, 0))], g_out=hd)

        da = functools.partial(_decode_attn, tok0=t_p, n_new=n_new, cp=cp)
        om_s = da(functools.partial(_dec_mla_kernel, layer=i, cp=cp, n_pages=n_pages,
                                    scale=mla_qk ** -0.5, rope=mla_rope),
                  page_table, qm, [c_new, kpe_new], [], [cache_mla_ckv, kpe_t], [], g_out=hm)
        of_s = da(functools.partial(_dec_fox_kernel, layer=i, cp=cp, n_pages=n_pages, scale=fox_dim ** -0.5),
                  page_table, qf, [kf_new, vf_new, logf_new], [], [cache_fox_k, cache_fox_v, logf_t],
                  [pltpu.VMEM((hf, 1), _F32), pltpu.VMEM((n_new, _LANE), _F32)], g_out=hf)
        lam_spec2 = pl.BlockSpec((8, _LANE), lambda b, c, pt: (0, 0))
        od_s = da(functools.partial(_dec_diff_kernel, layer=i, cp=cp, n_pages=n_pages,
                                    scale=dqk ** -0.5, lam_init=lam_init),
                  page_table, qd, [kd_new, vd_new], [(lamv, lam_spec2)], [dk_t, cache_diff_v], [], g_out=hd)

        om = jnp.concatenate([om_p, om_s], axis=1)
        of = jnp.concatenate([of_p, of_s], axis=1)
        od = jnp.concatenate([od_p, od_s], axis=1)
        h = _merge(h, om, of, od, wuv, g_diff_sub[i][None], w_out[i].astype(_BF16), tm=tm,
                   sub_scale=1.0 - lam_init)
        h = _ffn(h, g_ffn[i][None], w_gate[i].astype(_BF16), w_up[i].astype(_BF16),
                 w_down[i].astype(_BF16), tm=tm_ffn, tf=tf)
        p = jnp.concatenate([p_prompt[i].reshape(t_p, -1), p_sample[i].reshape(t_s, -1)], axis=0)
        h = _ple(h, p, g_ple[i][None], w_ple_gate[i].astype(_BF16), w_ple_proj[i].astype(_BF16), tm=tm)

        for j, r in enumerate((c_new, kpe_new[:, :mla_rope], kf_new, vf_new, logf_new[:, :hf], kd_new, vd_new)):
            rows[j].append(r)

    outs = [h[:t_p].reshape(batch, seq, d), h[t_p:].reshape(dec_batch, n_new, d)]
    for j, rs in enumerate(rows):
        st = jnp.stack(rs)
        tail = (2, dqk) if j == 5 else (st.shape[-1],)
        outs.append(st[:, :t_p].reshape((depth, batch, seq) + tail))
        outs.append(st[:, t_p:].reshape((depth, dec_batch, n_new) + tail))
    return tuple(outs)
```
